```python
import jax, jax.numpy as jnp
from jax import lax
import numpy as np

D_MODEL = 1024
BATCH = 8
SEQ = 4096
DEPTH = 2

CTX_LEN = 256
GRID_W = 64
D_GROUP = D_MODEL // 4
HG_HEADS = 4
HG_DK = D_GROUP // HG_HEADS
HG_DV = D_GROUP // HG_HEADS
HG_CHUNK = 16
RET_HEADS = 4
RET_DK = D_GROUP // RET_HEADS
RET_DV = D_GROUP // RET_HEADS
RET_CHUNK = 64
ROPE_BASE = 10000.0
GDN_HEADS = 4
GDN_DK = D_GROUP // GDN_HEADS
GDN_DV = D_GROUP // GDN_HEADS
GDN_CHUNK = 64
CONV_K = 3
S5_GROUP = 16
S5_GROUPS = D_GROUP // S5_GROUP
S5_STATE = 64
D_FF = 4 * D_MODEL
N_MOD = 6
EPS = 1e-6
LB_FLOOR = 1e-30
IN_SIZES = (D_GROUP, D_GROUP, D_GROUP, 2 * D_GROUP,
            D_GROUP, D_GROUP, D_GROUP, D_GROUP,
            3 * D_GROUP, D_GROUP, 2 * GDN_HEADS, 2 * GDN_HEADS,
            D_GROUP)
D_IN = sum(IN_SIZES)

kernel_name = 'hybrid_parallel_heads_dit_block'

F32 = jnp.float32


def rmsnorm(x, g):
    xf = x.astype(F32)
    return xf * lax.rsqrt(jnp.mean(xf * xf, axis=-1, keepdims=True) + EPS) * g.astype(F32)


def split_heads(a, h):
    return a.reshape(a.shape[:-1] + (h, a.shape[-1] // h))


def flip_t(a):
    return jnp.flip(a, axis=1)


def to_chunks(a, c):
    b, t = a.shape[:2]
    a = a.reshape((b, t // c, c) + a.shape[2:])
    return jnp.moveaxis(a, (1, 3), (0, 2))


def from_chunks(a):
    a = jnp.moveaxis(a, (0, 2), (1, 3))
    return a.reshape((a.shape[0], a.shape[1] * a.shape[2]) + a.shape[3:])


def gated_head_norm(o, gate, g):
    y = o * lax.rsqrt(jnp.mean(o * o, axis=-1, keepdims=True) + EPS)
    if g is not None:
        y = y * g.astype(F32)
    return y.reshape(gate.shape) * jax.nn.silu(gate)


def l2norm(a):
    return a * lax.rsqrt(jnp.sum(a * a, axis=-1, keepdims=True) + EPS)


def bidirectional_prefix_scan(core, ctx_seq, lat_seq, par, s0):
    outs_c, outs_l = [], []
    for d in range(2):
        tf = (lambda a: a) if d == 0 else flip_t
        oc, s_ctx = core(tuple(tf(a) for a in ctx_seq[d]), par[d], s0)
        ol, _ = core(tuple(tf(a) for a in lat_seq[d]), par[d], s_ctx)
        outs_c.append(tf(oc))
        outs_l.append(tf(ol))
    return outs_c[0] + outs_c[1], outs_l[0] + outs_l[1]


def gla_chunked(seq, par, s0):
    q, v, log_f = seq
    k = -jnp.expm1(log_f)
    bsz, t, h, _ = q.shape
    c = HG_CHUNK
    q, k, v, b = (to_chunks(a, c) for a in (q, k, v, log_f))
    b = jnp.cumsum(b, axis=3)
    tri = jnp.tril(jnp.ones((c, c), bool))[:, :, None]
    diff = b[..., :, None, :] - b[..., None, :, :]
    decay = jnp.where(tri, jnp.exp(jnp.minimum(diff, 0.0)), 0.0)
    attn = jnp.einsum('nbhik,nbhjk,nbhijk->nbhij', q, k, decay)
    o_intra = jnp.einsum('nbhij,nbhjv->nbhiv', attn, v)
    q_dec = q * jnp.exp(b)
    k_dec = k * jnp.exp(b[..., -1:, :] - b)
    c_dec = jnp.exp(b[..., -1, :])
    kv = jnp.einsum('nbhjk,nbhjv->nbhkv', k_dec, v)

    def step(s, xs):
        kv_n, cd_n = xs
        return s * cd_n[..., None] + kv_n, s

    s_fin, s_prev = lax.scan(step, s0, (kv, c_dec))
    o = o_intra + jnp.einsum('nbhik,nbhkv->nbhiv', q_dec, s_prev)
    return from_chunks(o), s_fin


def forget_log(f_raw, lb):
    return jnp.logaddexp(jnp.log(jnp.maximum(lb, LB_FLOOR)) + jax.nn.log_sigmoid(-f_raw),
                         jax.nn.log_sigmoid(f_raw))


def hgrn2_mixer(pc, pl, lb, norm_g):
    def prep(p):
        q, i, g, f = p
        q = split_heads(jax.nn.silu(q), HG_HEADS) * HG_DK ** -0.5
        v = split_heads(i, HG_HEADS)
        f_dir = jnp.split(f, 2, axis=-1)
        seqs = tuple((q, v, split_heads(forget_log(f_dir[d], lb[d]), HG_HEADS)) for d in range(2))
        return seqs, g
    sc, gc = prep(pc)
    sl, gl = prep(pl)
    s0 = jnp.zeros((pc[0].shape[0], HG_HEADS, HG_DK, HG_DV), F32)
    oc, ol = bidirectional_prefix_scan(gla_chunked, sc, sl, ((), ()), s0)
    return gated_head_norm(oc, gc, norm_g), gated_head_norm(ol, gl, norm_g)


def rope(x, pos):
    half = x.shape[-1] // 2
    inv = ROPE_BASE ** (-jnp.arange(half, dtype=F32) / half)
    ang = pos.astype(F32)[:, None] * inv[None, :]
    cos = jnp.cos(ang)[None, :, None, :]
    sin = jnp.sin(ang)[None, :, None, :]
    x1, x2 = x[..., :half], x[..., half:]
    return jnp.concatenate([x1 * cos - x2 * sin, x1 * sin + x2 * cos], axis=-1)


def retention_chunked(seq, par, s0):
    q, k, v = seq
    (log_g,) = par
    c = RET_CHUNK
    q, k, v = (to_chunks(a, c) for a in (q, k, v))
    idx = jnp.arange(c, dtype=F32)
    lg = log_g[:, None]
    rel = idx[:, None] - idx[None, :]
    dmat = jnp.where(rel >= 0, jnp.exp(jnp.maximum(rel, 0.0)[None] * lg[..., None]), 0.0)
    attn = jnp.einsum('nbhik,nbhjk->nbhij', q, k) * dmat
    o_intra = jnp.einsum('nbhij,nbhjv->nbhiv', attn, v)
    q_dec = jnp.exp((idx + 1.0)[None, :] * lg)
    k_dec = jnp.exp((c - 1.0 - idx)[None, :] * lg)
    c_dec = jnp.exp(c * log_g)
    kv = jnp.einsum('nbhjk,hj,nbhjv->nbhkv', k, k_dec, v)

    def step(s, kv_n):
        return s * c_dec[:, None, None] + kv_n, s

    s_fin, s_prev = lax.scan(step, s0, kv)
    o = o_intra + jnp.einsum('nbhik,hi,nbhkv->nbhiv', q, q_dec, s_prev)
    return from_chunks(o), s_fin


def retention_mixer(pc, pl, log_gamma, pos_c, pos_l):
    def prep(p, pos):
        q, k, v, g = p
        q = rope(split_heads(q, RET_HEADS), pos)
        k = rope(split_heads(k, RET_HEADS), pos) * RET_DK ** -0.5
        s = (q, k, split_heads(v, RET_HEADS))
        return (s, s), g
    sc, gc = prep(pc, pos_c)
    sl, gl = prep(pl, pos_l)
    s0 = jnp.zeros((pc[0].shape[0], RET_HEADS, RET_DK, RET_DV), F32)
    par = ((log_gamma[0],), (log_gamma[1],))
    oc, ol = bidirectional_prefix_scan(retention_chunked, sc, sl, par, s0)
    return gated_head_norm(oc, gc, None), gated_head_norm(ol, gl, None)


def gated_delta_chunked(seq, par, s0):
    c = GDN_CHUNK
    q, k, v, log_a, beta = (to_chunks(a, c) for a in seq)
    dv = v.shape[-1]
    g = jnp.cumsum(log_a, axis=-1)
    tri = jnp.tril(jnp.ones((c, c), bool))
    strict = jnp.tril(jnp.ones((c, c), bool), -1)
    diff = g[..., :, None] - g[..., None, :]
    lmat = jnp.where(tri, jnp.exp(jnp.minimum(diff, 0.0)), 0.0)
    kb = k * beta[..., None]
    a_mat = jnp.where(strict, jnp.einsum('nbhik,nbhjk->nbhij', kb, k) * lmat, 0.0)
    rhs = jnp.concatenate([v * beta[..., None], kb * jnp.exp(g)[..., None]], axis=-1)
    uw = lax.linalg.triangular_solve(a_mat + jnp.eye(c, dtype=F32), rhs, left_side=True, lower=True)
    u, w = uw[..., :dv], uw[..., dv:]
    qk = jnp.where(tri, jnp.einsum('nbhik,nbhjk->nbhij', q, k) * lmat, 0.0)
    q_dec = q * jnp.exp(g)[..., None]
    k_dec = k * jnp.exp(g[..., -1:] - g)[..., None]
    c_dec = jnp.exp(g[..., -1])

    def step(s, xs):
        u_n, w_n, qk_n, qd_n, kd_n, cd_n = xs
        v_new = u_n - jnp.einsum('bhck,bhkv->bhcv', w_n, s)
        o_n = jnp.einsum('bhck,bhkv->bhcv', qd_n, s) + jnp.einsum('bhij,bhjv->bhiv', qk_n, v_new)
        s = s * cd_n[..., None, None] + jnp.einsum('bhck,bhcv->bhkv', kd_n, v_new)
        return s, o_n

    s_fin, o = lax.scan(step, s0, (u, w, qk, q_dec, k_dec, c_dec))
    return from_chunks(o), s_fin


def depthwise_conv2d(x, w):
    ch = x.shape[-1]
    return lax.conv_general_dilated(x, w[:, :, None, :], window_strides=(1, 1), padding='SAME',
                                    dimension_numbers=('NHWC', 'HWIO', 'NHWC'),
                                    feature_group_count=ch)


def gdn_mixer(pc, pl, rows, conv_w, a_log, dt_bias, norm_g):
    conv_w = conv_w.astype(F32)

    def prep(p, n_rows):
        qkv, g, a, b = p
        bsz, t, ch = qkv.shape
        grid = qkv.reshape(bsz, n_rows, t // n_rows, ch)
        qkv = jax.nn.silu(depthwise_conv2d(grid, conv_w)).reshape(bsz, t, ch)
        q, k, v = jnp.split(qkv, 3, axis=-1)
        q = l2norm(split_heads(q, GDN_HEADS)) * GDN_DK ** -0.5
        k = l2norm(split_heads(k, GDN_HEADS))
        v = split_heads(v, GDN_HEADS)
        a_dir = jnp.split(a, 2, axis=-1)
        b_dir = jnp.split(b, 2, axis=-1)
        seqs = tuple((q, k, v,
                      -jnp.exp(a_log[d]) * jax.nn.softplus(a_dir[d] + dt_bias[d]),
                      jax.nn.sigmoid(b_dir[d])) for d in range(2))
        return seqs, g
    sc, gc = prep(pc, 1)
    sl, gl = prep(pl, rows)
    s0 = jnp.zeros((pc[0].shape[0], GDN_HEADS, GDN_DK, GDN_DV), F32)
    oc, ol = bidirectional_prefix_scan(gated_delta_chunked, sc, sl, ((), ()), s0)
    return gated_head_norm(oc, gc, norm_g), gated_head_norm(ol, gl, norm_g)


def s5_core(seq, par, s0):
    (u,) = seq
    lam_re, lam_im, log_dt, b_re, b_im, c_re, c_im = par
    bsz, t, _ = u.shape
    ug = u.reshape(bsz, t, S5_GROUPS, S5_GROUP)
    dt = jnp.exp(log_dt)[:, None]
    mag = jnp.exp(lam_re * dt)
    ar, ai = mag * jnp.cos(lam_im * dt), mag * jnp.sin(lam_im * dt)
    den = lam_re * lam_re + lam_im * lam_im
    nr, ni = ar - 1.0, ai
    fr = (nr * lam_re + ni * lam_im) / den
    fi = (ni * lam_re - nr * lam_im) / den
    bbr = fr[..., None] * b_re - fi[..., None] * b_im
    bbi = fr[..., None] * b_im + fi[..., None] * b_re
    xr = jnp.einsum('gpc,btgc->tbgp', bbr, ug)
    xi = jnp.einsum('gpc,btgc->tbgp', bbi, ug)
    h0r, h0i = s0
    xr = xr.at[0].add(ar * h0r - ai * h0i)
    xi = xi.at[0].add(ar * h0i + ai * h0r)
    a_r = jnp.broadcast_to(ar, (t, 1) + ar.shape)
    a_i = jnp.broadcast_to(ai, (t, 1) + ai.shape)

    def combine(e1, e2):
        a1r, a1i, b1r, b1i = e1
        a2r, a2i, b2r, b2i = e2
        return (a2r * a1r - a2i * a1i, a2r * a1i + a2i * a1r,
                a2r * b1r - a2i * b1i + b2r, a2r * b1i + a2i * b1r + b2i)

    _, _, hr, hi = lax.associative_scan(combine, (a_r, a_i, xr, xi), axis=0)
    y = jnp.einsum('gcp,tbgp->btgc', c_re, hr) - jnp.einsum('gcp,tbgp->btgc', c_im, hi)
    return y.reshape(bsz, t, D_GROUP), (hr[-1], hi[-1])


def s5_mixer(uc, ul, lam_re, lam_im, log_dt, b_re, b_im, c_re, c_im, d_skip, glu_w, glu_b):
    f = lambda a: a.astype(F32)
    par = tuple((f(lam_re[d]), f(lam_im[d]), f(log_dt[d]), f(b_re), f(b_im), f(c_re), f(c_im))
                for d in range(2))
    zero = jnp.zeros((uc.shape[0], S5_GROUPS, S5_STATE), F32)
    oc, ol = bidirectional_prefix_scan(s5_core, ((uc,), (uc,)), ((ul,), (ul,)), par, (zero, zero))

    def finish(y, u):
        y = jax.nn.gelu(y + u * f(d_skip))
        return y * jax.nn.sigmoid(y @ f(glu_w) + f(glu_b))
    return finish(oc, uc), finish(ol, ul)


def split_cols(z):
    parts = jnp.split(z.astype(F32), np.cumsum(IN_SIZES)[:-1].tolist(), axis=-1)
    hq, hi, hg, hf, rq, rk, rv, rg, gqkv, gg, ga, gb, su = parts
    return (hq, hi, hg, hf), (rq, rk, rv, rg), (gqkv, gg, ga, gb), su


def hybrid_mixer(hc, hl, rows, pos_c, pos_l, need_ctx, w_in, lb, hg_norm_g, ret_log_gamma,
                 conv_w, a_log, dt_bias, gdn_norm_g, lam_re, lam_im, log_dt, b_re, b_im,
                 c_re, c_im, d_skip, glu_w, glu_b, w_out):
    pc = split_cols(hc @ w_in)
    pl = split_cols(hl @ w_in)
    a_c, a_l = hgrn2_mixer(pc[0], pl[0], lb, hg_norm_g)
    r_c, r_l = retention_mixer(pc[1], pl[1], ret_log_gamma, pos_c, pos_l)
    g_c, g_l = gdn_mixer(pc[2], pl[2], rows, conv_w, a_log.astype(F32), dt_bias.astype(F32), gdn_norm_g)
    s_c, s_l = s5_mixer(pc[3], pl[3], lam_re, lam_im, log_dt, b_re, b_im, c_re, c_im, d_skip, glu_w, glu_b)
    y_l = jnp.concatenate([a_l, r_l, g_l, s_l], axis=-1) @ w_out.astype(F32)
    y_c = jnp.concatenate([a_c, r_c, g_c, s_c], axis=-1) @ w_out.astype(F32) if need_ctx else None
    return y_c, y_l


def modulation(cvec, w, b):
    m = jax.nn.silu(cvec.reshape(-1, cvec.shape[-1]).astype(F32)) @ w.astype(F32) + b.astype(F32)
    return jnp.split(m[:, None, :], N_MOD, axis=-1)


def modulate(h, shift, scale):
    return h * (1.0 + scale) + shift


def sqrelu_mlp(h, w1, w2):
    return jnp.square(jax.nn.relu(h @ w1.astype(F32))) @ w2.astype(F32)


def setup_inputs(seed: int = 0) -> dict:
    key = jax.random.key(seed)
    ks = jax.random.split(key, 32)
    L, D, G, P = DEPTH, D_MODEL, S5_GROUPS, S5_STATE

    def nrm(k, shape, s=1.0):
        return s * jax.random.normal(k, shape, F32)

    ret_logit = jnp.log(2.0 ** (5.0 + jnp.arange(RET_HEADS, dtype=F32)) - 1.0)
    lo, hi = float(np.log(1e-3)), float(np.log(1e-1))
    gdn_dt = jnp.exp(jax.random.uniform(ks[16], (L, 2, GDN_HEADS), F32, lo, hi))
    return {
        'x': nrm(ks[0], (BATCH, SEQ, D)),
        'c': nrm(ks[1], (BATCH, D)),
        'ctx': nrm(ks[2], (BATCH, CTX_LEN, D)),
        'c_ctx': nrm(ks[3], (D,)),
        'mod_w': nrm(ks[4], (L, D, N_MOD * D), 0.5 * D ** -0.5),
        'mod_b': nrm(ks[5], (L, N_MOD * D), 0.02),
        'norm1_g': 1.0 + nrm(ks[6], (L, D), 0.02),
        'norm2_g': 1.0 + nrm(ks[7], (L, D), 0.02),
        'w_in': nrm(ks[8], (L, D, D_IN), D ** -0.5),
        'hgrn_lb_logits': nrm(ks[9], (L, 2, D_GROUP), 0.1),
        'hgrn_norm_g': 1.0 + nrm(ks[10], (L, HG_DV), 0.02),
        'ret_decay_logit': ret_logit + nrm(ks[11], (L, 2, RET_HEADS), 0.05),
        'gdn_conv_w': nrm(ks[12], (L, CONV_K, CONV_K, 3 * D_GROUP), 1.0 / CONV_K),
        'gdn_a_log': jnp.log(jax.random.uniform(ks[13], (L, 2, GDN_HEADS), F32, 1.0, 16.0)),
        'gdn_dt_bias': gdn_dt + jnp.log(-jnp.expm1(-gdn_dt)),
        'gdn_norm_g': 1.0 + nrm(ks[14], (L, GDN_DV), 0.02),
        's5_lam_re': -0.5 + nrm(ks[15], (L, 2, G, P), 0.01),
        's5_lam_im': jnp.pi * jnp.arange(P, dtype=F32) + nrm(ks[17], (L, 2, G, P), 0.01),
        's5_log_dt': jax.random.uniform(ks[18], (L, 2, G), F32, lo, hi),
        's5_b_re': nrm(ks[19], (L, G, P, S5_GROUP), (2 * S5_GROUP) ** -0.5),
        's5_b_im': nrm(ks[20], (L, G, P, S5_GROUP), (2 * S5_GROUP) ** -0.5),
        's5_c_re': nrm(ks[21], (L, G, S5_GROUP, P), P ** -0.5),
        's5_c_im': nrm(ks[22], (L, G, S5_GROUP, P), P ** -0.5),
        's5_d': nrm(ks[23], (L, D_GROUP)),
        's5_glu_w': nrm(ks[24], (L, D_GROUP, D_GROUP), D_GROUP ** -0.5),
        's5_glu_b': nrm(ks[25], (L, D_GROUP), 0.02),
        'w_out': nrm(ks[26], (L, D, D), D ** -0.5),
        'mlp_w1': nrm(ks[27], (L, D, D_FF), D ** -0.5),
        'mlp_w2': nrm(ks[28], (L, D_FF, D), D_FF ** -0.5),
        'final_norm_g': 1.0 + nrm(ks[29], (D,), 0.02),
    }


def reference(x, c, ctx, c_ctx, mod_w, mod_b, norm1_g, norm2_g, w_in, hgrn_lb_logits, hgrn_norm_g,
              ret_decay_logit, gdn_conv_w, gdn_a_log, gdn_dt_bias, gdn_norm_g, s5_lam_re, s5_lam_im,
              s5_log_dt, s5_b_re, s5_b_im, s5_c_re, s5_c_im, s5_d, s5_glu_w, s5_glu_b, w_out,
              mlp_w1, mlp_w2, final_norm_g):
    t_lat, t_ctx = x.shape[1], ctx.shape[1]
    rows = t_lat // GRID_W
    pos_c = jnp.arange(t_ctx)
    pos_l = t_ctx + jnp.arange(t_lat)
    sm = jax.nn.softmax(hgrn_lb_logits.astype(F32), axis=0)
    lbs = jnp.cumsum(sm, axis=0) - sm[:1]
    for l in range(DEPTH):
        need_ctx = l < DEPTH - 1
        ml = modulation(c, mod_w[l], mod_b[l])
        mc = modulation(c_ctx, mod_w[l], mod_b[l])
        hl = modulate(rmsnorm(x, norm1_g[l]), ml[0], ml[1])
        hc = modulate(rmsnorm(ctx, norm1_g[l]), mc[0], mc[1])
        y_c, y_l = hybrid_mixer(
            hc, hl, rows, pos_c, pos_l, need_ctx, w_in[l].astype(F32), lbs[l], hgrn_norm_g[l],
            jax.nn.log_sigmoid(ret_decay_logit[l].astype(F32)), gdn_conv_w[l], gdn_a_log[l],
            gdn_dt_bias[l], gdn_norm_g[l], s5_lam_re[l], s5_lam_im[l], s5_log_dt[l], s5_b_re[l],
            s5_b_im[l], s5_c_re[l], s5_c_im[l], s5_d[l], s5_glu_w[l], s5_glu_b[l], w_out[l])
        x = x + (ml[2] * y_l).astype(x.dtype)
        h2 = modulate(rmsnorm(x, norm2_g[l]), ml[3], ml[4])
        x = x + (ml[5] * sqrelu_mlp(h2, mlp_w1[l], mlp_w2[l])).astype(x.dtype)
        if need_ctx:
            ctx = ctx + (mc[2] * y_c).astype(ctx.dtype)
            h2c = modulate(rmsnorm(ctx, norm2_g[l]), mc[3], mc[4])
            ctx = ctx + (mc[5] * sqrelu_mlp(h2c, mlp_w1[l], mlp_w2[l])).astype(ctx.dtype)
    return rmsnorm(x, final_norm_g).astype(x.dtype)
```

```python
import functools

import jax
import jax.numpy as jnp
import numpy as np
from jax import lax
from jax.experimental import pallas as pl
from jax.experimental.pallas import tpu as pltpu

F32 = jnp.float32
BF16 = jnp.bfloat16

D_MODEL = 1024
D_GROUP = D_MODEL // 4
N_HEADS = 4
HEAD_DIM = D_GROUP // N_HEADS
D_FF = 4 * D_MODEL
N_MOD = 6
EPS = 1e-6
LB_FLOOR = 1e-30
GRID_W = 64
ROPE_BASE = 10000.0
S5_GROUP = 16
S5_GROUPS = D_GROUP // S5_GROUP
S5_STATE = 64
S5_LANES = S5_GROUPS * S5_STATE

BLK = 256
HG_CHUNK = 16
GDN_CHUNK = 64
GDN_SUB = 16
S5_BLK = 128
FF_CHUNK = 1024
V7X_VMEM_LIMIT = 56 * 1024 * 1024

ZA_W = 5 * D_GROUP
ZR_W = 4 * D_GROUP
ZG_W = 4 * D_GROUP + 128
ZS_W = D_GROUP
Z_W = ZA_W + ZR_W + ZG_W + ZS_W


def _cparams(sem):
    return pltpu.CompilerParams(dimension_semantics=sem, vmem_limit_bytes=V7X_VMEM_LIMIT)


def _bwd_block(s, nbc, nb):
    return jnp.where(s < nbc, nbc - 1 - s, nb - 1 - (s - nbc))


def _dot(a, b):
    return jnp.dot(a, b, preferred_element_type=F32)


def _dot_nt(a, b):
    return lax.dot_general(a, b, (((1,), (1,)), ((), ())), preferred_element_type=F32)


def _dot_tn(a, b):
    return lax.dot_general(a, b, (((0,), (0,)), ((), ())), preferred_element_type=F32)


def _bmm(a, b):
    return jnp.einsum('gij,gjk->gik', a, b, preferred_element_type=F32)


def _bmm_nt(a, b):
    return jnp.einsum('gik,gjk->gij', a, b, preferred_element_type=F32)


def _silu(x):
    return x * jax.nn.sigmoid(x)


def _head_eq(n):
    r = lax.shift_right_logical(lax.broadcasted_iota(jnp.int32, (n, n), 0), 6)
    c = lax.shift_right_logical(lax.broadcasted_iota(jnp.int32, (n, n), 1), 6)
    return r == c


def _seg_sum(x, e):
    hi = x.astype(BF16)
    lo = (x - hi.astype(F32)).astype(BF16)
    return _dot(hi, e) + _dot(lo, e)


def _rms_mod(x, g, shift, scale):
    h = x * lax.rsqrt(jnp.mean(x * x, axis=-1, keepdims=True) + EPS) * g
    return h * (1.0 + scale) + shift


def _chunk_cumsum(x, pos, c, rev):
    n = x.shape[0]
    sh = 1
    while sh < c:
        if rev:
            x = x + jnp.where(pos < c - sh, pltpu.roll(x, n - sh, 0), 0.0)
        else:
            x = x + jnp.where(pos >= sh, pltpu.roll(x, sh, 0), 0.0)
        sh *= 2
    return x


def _chunk_total(x, pos, c):
    n = x.shape[0]
    sh = 1
    while sh < c:
        x = x + jnp.where(pos >= sh, pltpu.roll(x, sh, 0), pltpu.roll(x, n - (c - sh), 0))
        sh *= 2
    return x


def _mod_kernel(c_ref, w_ref, b_ref, o_ref):
    o_ref[0] = _dot(_silu(c_ref[...]), w_ref[0]) + b_ref[0]


def _mod_proj(cvecs, mod_w, mod_b):
    depth, d, n = mod_w.shape
    rows = cvecs.shape[0]
    tn = 1536
    return pl.pallas_call(
        _mod_kernel,
        grid=(depth, n // tn),
        in_specs=[pl.BlockSpec((rows, d), lambda l, j: (0, 0)),
                  pl.BlockSpec((1, d, tn), lambda l, j: (l, 0, j)),
                  pl.BlockSpec((1, 1, tn), lambda l, j: (l, 0, j))],
        out_specs=pl.BlockSpec((1, rows, tn), lambda l, j: (l, 0, j)),
        out_shape=jax.ShapeDtypeStruct((depth, rows, n), F32),
        compiler_params=_cparams(("arbitrary", "arbitrary")),
        name="mod_proj",
    )(cvecs, mod_w, mod_b.reshape(depth, 1, n))


def _in_proj_kernel(x_ref, mod_ref, g_ref, w_ref, za_ref, zr_ref, zg_ref, zs_ref):
    m = mod_ref[0, 0]
    h = _rms_mod(x_ref[0], g_ref[...], m[0:1], m[1:2]).astype(BF16)
    o = 0
    for ref, width in ((za_ref, ZA_W), (zr_ref, ZR_W), (zg_ref, ZG_W), (zs_ref, ZS_W)):
        ref[0] = _dot(h, w_ref[:, o:o + width])
        o += width


def _in_proj(xc, modv, g, w, nbc):
    b, l, d = xc.shape
    nb = l // BLK
    row = lambda width: pl.BlockSpec((1, BLK, width), lambda i, j: (i, j, 0))
    return pl.pallas_call(
        _in_proj_kernel,
        grid=(b, nb),
        in_specs=[row(d),
                  pl.BlockSpec((1, 1, 8, d), lambda i, j: (i, jnp.where(j < nbc, 0, 1), 0, 0)),
                  pl.BlockSpec((1, d), lambda i, j: (0, 0)),
                  pl.BlockSpec((d, Z_W), lambda i, j: (0, 0))],
        out_specs=[row(ZA_W), row(ZR_W), row(ZG_W), row(ZS_W)],
        out_shape=[jax.ShapeDtypeStruct((b, l, width), F32) for width in (ZA_W, ZR_W, ZG_W, ZS_W)],
        compiler_params=_cparams(("arbitrary", "arbitrary")),
        name="in_proj",
    )(xc, modv, g, w)


def _hgrn_dir(q_raw, v, f_raw, lb, st, rev):
    n = q_raw.shape[0]
    c = HG_CHUNK
    pos = lax.broadcasted_iota(jnp.int32, (n, D_GROUP), 0) & (c - 1)
    heq = _head_eq(D_GROUP)
    e = heq.astype(BF16)
    qs = _silu(q_raw) * HEAD_DIM ** -0.5
    lbm = jnp.maximum(lb, LB_FLOOR)
    ex = jnp.exp(-jnp.abs(f_raw))
    inv = 1.0 / (1.0 + ex)
    sig_pos = jnp.where(f_raw >= 0, inv, ex * inv)
    sig_neg = jnp.where(f_raw >= 0, ex * inv, inv)
    logf = jnp.log(lbm * sig_neg + sig_pos)
    kk = (1.0 - lbm) * sig_neg
    bcum = _chunk_cumsum(logf, pos, c, rev)
    tot = _chunk_total(logf, pos, c)

    o = _seg_sum(qs * kk, e) * v
    for r in range(1, c):
        sh = n - r if rev else r
        bs = pltpu.roll(bcum, sh, 0)
        ks = pltpu.roll(kk, sh, 0)
        vs = pltpu.roll(v, sh, 0)
        att = _seg_sum(qs * ks * jnp.exp(jnp.minimum(bcum - bs, 0.0)), e)
        valid = (pos < c - r) if rev else (pos >= r)
        o = o + jnp.where(valid, att * vs, 0.0)

    qd = qs * jnp.exp(bcum)
    kd = kk * jnp.exp(tot - bcum)
    outs = [None] * (n // c)
    order = range(n // c - 1, -1, -1) if rev else range(n // c)
    for ci in order:
        rows = slice(ci * c, (ci + 1) * c)
        outs[ci] = _dot_nt(qd[rows], st)
        kv = _dot_tn(v[rows], kd[rows])
        st = st * jnp.exp(tot[ci * c:ci * c + 1, :]) + jnp.where(heq, kv, 0.0)
    return o + jnp.concatenate(outs, axis=0), st


def _hgrn_kernel(qf_ref, vf_ref, ff_ref, qb_ref, vb_ref, fb_ref, lb_ref, of_ref, ob_ref, st_ref):
    @pl.when(pl.program_id(1) == 0)
    def _():
        st_ref[...] = jnp.zeros_like(st_ref)

    o, st = _hgrn_dir(qf_ref[0], vf_ref[0], ff_ref[0], lb_ref[0:1, :], st_ref[0], False)
    of_ref[0] = o
    st_ref[0] = st
    o, st = _hgrn_dir(qb_ref[0], vb_ref[0], fb_ref[0], lb_ref[1:2, :], st_ref[1], True)
    ob_ref[0] = o
    st_ref[1] = st


def _scan_specs(nbc, nb, cols_f, cols_b, width=D_GROUP):
    fwd = [pl.BlockSpec((1, BLK, width), functools.partial(lambda i, s, c: (i, s, c), c=c)) for c in cols_f]
    bwd = [pl.BlockSpec((1, BLK, width), functools.partial(lambda i, s, c: (i, _bwd_block(s, nbc, nb), c), c=c))
           for c in cols_b]
    return fwd, bwd


def _hgrn_scan(za, lb, nbc):
    b, l, _ = za.shape
    nb = l // BLK
    fwd, bwd = _scan_specs(nbc, nb, (0, 1, 3), (0, 1, 4))
    outs = _scan_specs(nbc, nb, (0,), (0,))
    return pl.pallas_call(
        _hgrn_kernel,
        grid=(b, nb),
        in_specs=fwd + bwd + [pl.BlockSpec((2, D_GROUP), lambda i, s: (0, 0))],
        out_specs=outs[0] + outs[1],
        out_shape=[jax.ShapeDtypeStruct((b, l, D_GROUP), F32)] * 2,
        scratch_shapes=[pltpu.VMEM((2, D_GROUP, D_GROUP), F32)],
        compiler_params=_cparams(("arbitrary", "arbitrary")),
        name="hgrn_scan",
    )(za, za, za, za, za, za, lb)


def _rope(x, cos, sin_signed):
    lane = lax.broadcasted_iota(jnp.int32, x.shape, 1) & (HEAD_DIM - 1)
    half = HEAD_DIM // 2
    partner = jnp.where(lane < half, pltpu.roll(x, D_GROUP - half, 1), pltpu.roll(x, half, 1))
    return x * cos + partner * sin_signed


def _ret_dir(q, k, v, cos, sin, lg_ref, d, lg_lane, st, rev):
    n = q.shape[0]
    q = _rope(q, cos, sin)
    k = _rope(k, cos, sin) * HEAD_DIM ** -0.5
    ri = lax.broadcasted_iota(jnp.int32, (n, n), 0)
    ci = lax.broadcasted_iota(jnp.int32, (n, n), 1)
    rel = (ci - ri) if rev else (ri - ci)
    relf = jnp.maximum(rel, 0).astype(F32)
    t = lax.broadcasted_iota(jnp.int32, (n, D_GROUP), 0).astype(F32)
    lane_head = lax.shift_right_logical(lax.broadcasted_iota(jnp.int32, (n, D_GROUP), 1), 6)
    if rev:
        qdec, kdec = jnp.exp((n - t) * lg_lane), jnp.exp(t * lg_lane)
    else:
        qdec, kdec = jnp.exp((t + 1.0) * lg_lane), jnp.exp((n - 1.0 - t) * lg_lane)
    kb = k.astype(BF16)
    vb = v.astype(BF16)
    o = _dot((q * qdec).astype(BF16), st.astype(BF16))
    for h in range(N_HEADS):
        dmat = jnp.where(rel >= 0, jnp.exp(relf * lg_ref[d, h]), 0.0)
        s = _dot_nt(jnp.where(lane_head == h, q, 0.0).astype(BF16), kb)
        oh = _dot((s * dmat).astype(BF16), vb)
        o = o + jnp.where(lane_head == h, oh, 0.0)
    kv = _dot((k * kdec).T.astype(BF16), vb)
    st = st * jnp.exp(n * lg_lane) + jnp.where(_head_eq(D_GROUP), kv, 0.0)
    return o, st


def _ret_kernel(lg_ref, qf_ref, kf_ref, vf_ref, cf_ref, sf_ref, qb_ref, kb_ref, vb_ref, cb_ref, sb_ref,
                lgl_ref, of_ref, ob_ref, st_ref):
    @pl.when(pl.program_id(1) == 0)
    def _():
        st_ref[...] = jnp.zeros_like(st_ref)

    o, st = _ret_dir(qf_ref[0], kf_ref[0], vf_ref[0], cf_ref[...], sf_ref[...], lg_ref, 0,
                     lgl_ref[0:1, :], st_ref[0], False)
    of_ref[0] = o
    st_ref[0] = st
    o, st = _ret_dir(qb_ref[0], kb_ref[0], vb_ref[0], cb_ref[...], sb_ref[...], lg_ref, 1,
                     lgl_ref[1:2, :], st_ref[1], True)
    ob_ref[0] = o
    st_ref[1] = st


def _ret_scan(zr, cos_t, sin_t, lg, nbc):
    b, l, _ = zr.shape
    nb = l // BLK
    fwd, bwd = _scan_specs(nbc, nb, (0, 1, 2), (0, 1, 2))
    outs = _scan_specs(nbc, nb, (0,), (0,))
    tab_f = pl.BlockSpec((BLK, D_GROUP), lambda i, s: (s, 0))
    tab_b = pl.BlockSpec((BLK, D_GROUP), lambda i, s: (_bwd_block(s, nbc, nb), 0))
    lg_lane = jnp.repeat(lg, HEAD_DIM, axis=-1)
    return pl.pallas_call(
        _ret_kernel,
        grid=(b, nb),
        in_specs=[pl.BlockSpec(memory_space=pltpu.SMEM)] + fwd + [tab_f, tab_f] + bwd + [tab_b, tab_b]
                 + [pl.BlockSpec((2, D_GROUP), lambda i, s: (0, 0))],
        out_specs=outs[0] + outs[1],
        out_shape=[jax.ShapeDtypeStruct((b, l, D_GROUP), F32)] * 2,
        scratch_shapes=[pltpu.VMEM((2, D_GROUP, D_GROUP), F32)],
        compiler_params=_cparams(("arbitrary", "arbitrary")),
        name="ret_scan",
    )(lg, zr, zr, zr, cos_t, sin_t, zr, zr, zr, cos_t, sin_t, lg_lane)


def _gdn_prep_kernel(prev_ref, cur_ref, next_ref, ab_ref, cw_ref, an_ref, dtb_ref, qkv_ref, ga_ref, xs_ref,
                     *, nbc, nb):
    j = pl.program_id(1)
    is_ctx = j < nbc
    prev_ok = jnp.where(is_ctx, j > 0, j > nbc)
    next_ok = jnp.where(is_ctx, j < nbc - 1, j < nb - 1)
    n = BLK
    xs_ref[0:n, :] = jnp.where(prev_ok, prev_ref[0], 0.0)
    xs_ref[n:2 * n, :] = cur_ref[0]
    xs_ref[2 * n:3 * n, :] = jnp.where(next_ok, next_ref[0], 0.0)
    colpos = lax.broadcasted_iota(jnp.int32, (n, 1), 0) & (GRID_W - 1)
    acc = jnp.zeros((n, 3 * D_GROUP), F32)
    for dr in (-1, 0, 1):
        for dc in (-1, 0, 1):
            win = xs_ref[pl.ds(n + GRID_W * dr + dc, n), :]
            col_ok = (colpos >= 1) if dc == -1 else ((colpos <= GRID_W - 2) if dc == 1 else (colpos >= 0))
            ok = (is_ctx | col_ok) if dr == 0 else (jnp.logical_not(is_ctx) & col_ok)
            acc = acc + jnp.where(ok, win, 0.0) * cw_ref[(dr + 1) * 3 + (dc + 1):(dr + 1) * 3 + (dc + 2), :]
    xc = _silu(acc)
    e = _head_eq(D_GROUP).astype(BF16)
    q = xc[:, 0:D_GROUP]
    k = xc[:, D_GROUP:2 * D_GROUP]
    qkv_ref[0, :, 0:D_GROUP] = q * lax.rsqrt(_seg_sum(q * q, e) + EPS) * HEAD_DIM ** -0.5
    qkv_ref[0, :, D_GROUP:2 * D_GROUP] = k * lax.rsqrt(_seg_sum(k * k, e) + EPS)
    qkv_ref[0, :, 2 * D_GROUP:] = xc[:, 2 * D_GROUP:]
    ab = ab_ref[0]
    z = ab + dtb_ref[...]
    softplus = jnp.maximum(z, 0.0) + jnp.log(1.0 + jnp.exp(-jnp.abs(z)))
    lane = lax.broadcasted_iota(jnp.int32, ab.shape, 1)
    ga_ref[0] = jnp.where(lane < 2 * N_HEADS, an_ref[...] * softplus, jax.nn.sigmoid(ab))


def _gdn_prep(zg, conv_w, a_neg, dt_bias, nbc):
    b, l, _ = zg.shape
    nb = l // BLK
    w3 = 3 * D_GROUP
    return pl.pallas_call(
        functools.partial(_gdn_prep_kernel, nbc=nbc, nb=nb),
        grid=(b, nb),
        in_specs=[pl.BlockSpec((1, BLK, w3), lambda i, j: (i, jnp.maximum(j - 1, 0), 0)),
                  pl.BlockSpec((1, BLK, w3), lambda i, j: (i, j, 0)),
                  pl.BlockSpec((1, BLK, w3), lambda i, j: (i, jnp.minimum(j + 1, nb - 1), 0)),
                  pl.BlockSpec((1, BLK, 128), lambda i, j: (i, j, (ZG_W - 128) // 128)),
                  pl.BlockSpec((9, w3), lambda i, j: (0, 0)),
                  pl.BlockSpec((1, 128), lambda i, j: (0, 0)),
                  pl.BlockSpec((1, 128), lambda i, j: (0, 0))],
        out_specs=[pl.BlockSpec((1, BLK, w3), lambda i, j: (i, j, 0)),
                   pl.BlockSpec((1, BLK, 128), lambda i, j: (i, j, 0))],
        out_shape=[jax.ShapeDtypeStruct((b, l, w3), F32), jax.ShapeDtypeStruct((b, l, 128), F32)],
        scratch_shapes=[pltpu.VMEM((3 * BLK, w3), F32)],
        compiler_params=_cparams(("arbitrary", "arbitrary")),
        name="gdn_prep",
    )(zg, zg, zg, zg, conv_w, a_neg, dt_bias)


def _to_inst(x):
    nc = x.shape[0] // GDN_CHUNK
    parts = []
    for ci in range(nc):
        for h in range(N_HEADS):
            parts.append(x[ci * GDN_CHUNK:(ci + 1) * GDN_CHUNK, h * HEAD_DIM:(h + 1) * HEAD_DIM][None])
    return jnp.concatenate(parts, axis=0)


def _col_inst(x, lane0):
    nc = x.shape[0] // GDN_CHUNK
    parts = []
    for ci in range(nc):
        for h in range(N_HEADS):
            col = x[ci * GDN_CHUNK:(ci + 1) * GDN_CHUNK, lane0 + h:lane0 + h + 1]
            parts.append(jnp.broadcast_to(col, (GDN_CHUNK, HEAD_DIM))[None])
    return jnp.concatenate(parts, axis=0)


def _row_inst(xt, row0):
    nc = xt.shape[1] // GDN_CHUNK
    parts = []
    for ci in range(nc):
        for h in range(N_HEADS):
            row = xt[row0 + h:row0 + h + 1, ci * GDN_CHUNK:(ci + 1) * GDN_CHUNK]
            parts.append(jnp.broadcast_to(row, (HEAD_DIM, GDN_CHUNK))[None])
    return jnp.concatenate(parts, axis=0)


def _gdn_dir(qkv, ga, st, d, rev):
    n = qkv.shape[0]
    c = GDN_CHUNK
    nc = n // c
    pos = lax.broadcasted_iota(jnp.int32, ga.shape, 0) & (c - 1)
    g2 = _chunk_cumsum(ga, pos, c, rev)
    tot2 = _chunk_total(ga, pos, c)
    a0 = d * N_HEADS
    b0 = 2 * N_HEADS + d * N_HEADS
    gc = _col_inst(g2, a0)
    gr = _row_inst(g2.T, a0)
    totc = _col_inst(tot2, a0)
    beta = _col_inst(ga, b0)
    q3 = _to_inst(qkv[:, 0:D_GROUP])
    k3 = _to_inst(qkv[:, D_GROUP:2 * D_GROUP])
    v3 = _to_inst(qkv[:, 2 * D_GROUP:])

    ri = lax.broadcasted_iota(jnp.int32, (c, c), 0)
    ci = lax.broadcasted_iota(jnp.int32, (c, c), 1)
    tri = (ci >= ri) if rev else (ci <= ri)
    strict = (ci > ri) if rev else (ci < ri)
    same_sub = lax.shift_right_logical(ri, 4) == lax.shift_right_logical(ci, 4)
    eye = (ri == ci).astype(F32)

    lmat = jnp.where(tri, jnp.exp(jnp.minimum(gc - gr, 0.0)), 0.0)
    kb = k3 * beta
    amat = jnp.where(strict, _bmm_nt(kb, k3) * lmat, 0.0)
    dmat = jnp.where(same_sub, amat, 0.0)
    lo = amat - dmat
    d2 = _bmm(dmat, dmat)
    d4 = _bmm(d2, d2)
    d8 = _bmm(d4, d4)
    tdiag = _bmm(_bmm(eye - dmat, eye + d2), _bmm(eye + d4, eye + d8))
    mmat = _bmm(tdiag, lo)
    tinv = _bmm(_bmm(eye - mmat, eye + _bmm(mmat, mmat)), tdiag)
    u = _bmm(tinv, v3 * beta)
    w = _bmm(tinv, kb * jnp.exp(gc))
    qk = jnp.where(tri, _bmm_nt(q3, k3) * lmat, 0.0)
    qd = q3 * jnp.exp(gc)
    kd = k3 * jnp.exp(totc - gc)
    cdec = jnp.exp(totc)

    outs = [None] * nc
    order = range(nc - 1, -1, -1) if rev else range(nc)
    for cidx in order:
        sl = slice(cidx * N_HEADS, (cidx + 1) * N_HEADS)
        vnew = u[sl] - _bmm(w[sl], st)
        o = _bmm(qd[sl], st) + _bmm(qk[sl], vnew)
        st = st * cdec[sl] + jnp.einsum('gck,gcv->gkv', kd[sl], vnew, preferred_element_type=F32)
        outs[cidx] = jnp.concatenate([o[h] for h in range(N_HEADS)], axis=-1)
    return jnp.concatenate(outs, axis=0), st


def _gdn_kernel(xf_ref, gf_ref, xb_ref, gb_ref, of_ref, ob_ref, st_ref):
    @pl.when(pl.program_id(1) == 0)
    def _():
        st_ref[...] = jnp.zeros_like(st_ref)

    o, st = _gdn_dir(xf_ref[0], gf_ref[0], st_ref[0], 0, False)
    of_ref[0] = o
    st_ref[0] = st
    o, st = _gdn_dir(xb_ref[0], gb_ref[0], st_ref[1], 1, True)
    ob_ref[0] = o
    st_ref[1] = st


def _gdn_scan(qkv, ga, nbc):
    b, l, w3 = qkv.shape
    nb = l // BLK
    xf, xb = _scan_specs(nbc, nb, (0,), (0,), w3)
    gf, gb = _scan_specs(nbc, nb, (0,), (0,), 128)
    outs = _scan_specs(nbc, nb, (0,), (0,))
    return pl.pallas_call(
        _gdn_kernel,
        grid=(b, nb),
        in_specs=xf + gf + xb + gb,
        out_specs=outs[0] + outs[1],
        out_shape=[jax.ShapeDtypeStruct((b, l, D_GROUP), F32)] * 2,
        scratch_shapes=[pltpu.VMEM((2, N_HEADS, HEAD_DIM, HEAD_DIM), F32)],
        compiler_params=_cparams(("arbitrary", "arbitrary")),
        name="gdn_scan",
    )(qkv, ga, qkv, ga)


def _s5_kernel(u_ref, bm_ref, cm_ref, a_ref, *rest, rev, has_addend):
    if has_addend:
        add_ref, y_ref, x_ref, h_ref = rest
    else:
        y_ref, x_ref, h_ref = rest
    bsz = h_ref.shape[0]
    nt = u_ref.shape[0] // bsz

    @pl.when(pl.program_id(0) == 0)
    def _():
        h_ref[...] = jnp.zeros_like(h_ref)

    x_ref[...] = _dot(u_ref[...].astype(BF16), bm_ref[...])
    ar = a_ref[0]
    ai = a_ref[1]

    def step(i, carry):
        hr, hi = carry
        t = (nt - 1 - i) if rev else i
        rows = pl.ds(pl.multiple_of(t * bsz, bsz), bsz)
        nr = ar * hr - ai * hi + x_ref[rows, 0:S5_LANES]
        ni = ar * hi + ai * hr + x_ref[rows, S5_LANES:]
        x_ref[rows, 0:S5_LANES] = nr
        x_ref[rows, S5_LANES:] = ni
        return nr, ni

    hr, hi = lax.fori_loop(0, nt, step, (h_ref[:, 0:S5_LANES], h_ref[:, S5_LANES:]))
    h_ref[:, 0:S5_LANES] = hr
    h_ref[:, S5_LANES:] = hi
    y = _dot(x_ref[...].astype(BF16), cm_ref[...])
    y_ref[...] = y + add_ref[...] if has_addend else y


def _s5_scan(u_tb, bmat, cmat, a_bar, bsz, nbc_s, rev, addend=None):
    rows, w = u_tb.shape
    blk = S5_BLK * bsz
    nb = rows // blk
    idx = (lambda s: (_bwd_block(s, nbc_s, nb), 0)) if rev else (lambda s: (s, 0))
    const = lambda shape: pl.BlockSpec(shape, lambda s: (0,) * len(shape))
    in_specs = [pl.BlockSpec((blk, w), idx), const(bmat.shape), const(cmat.shape), const(a_bar.shape)]
    args = [u_tb, bmat, cmat, a_bar]
    if addend is not None:
        in_specs.append(pl.BlockSpec((blk, w), idx))
        args.append(addend)
    return pl.pallas_call(
        functools.partial(_s5_kernel, rev=rev, has_addend=addend is not None),
        grid=(nb,),
        in_specs=in_specs,
        out_specs=pl.BlockSpec((blk, w), idx),
        out_shape=jax.ShapeDtypeStruct((rows, w), F32),
        scratch_shapes=[pltpu.VMEM((blk, 2 * S5_LANES), F32), pltpu.VMEM((bsz, 2 * S5_LANES), F32)],
        compiler_params=_cparams(("arbitrary",)),
        name="s5_scan_bwd" if rev else "s5_scan_fwd",
    )(*args)


def _s5_discretize(lam_re, lam_im, log_dt, b_re, b_im, c_re, c_im, bsz):
    dt = jnp.exp(log_dt)[:, None]
    mag = jnp.exp(lam_re * dt)
    ar, ai = mag * jnp.cos(lam_im * dt), mag * jnp.sin(lam_im * dt)
    den = lam_re * lam_re + lam_im * lam_im
    nr, ni = ar - 1.0, ai
    fr = (nr * lam_re + ni * lam_im) / den
    fi = (ni * lam_re - nr * lam_im) / den
    bbr = fr[..., None] * b_re - fi[..., None] * b_im
    bbi = fr[..., None] * b_im + fi[..., None] * b_re
    eye = jnp.eye(S5_GROUPS, dtype=F32)
    expand_b = lambda m: jnp.einsum('gpc,gh->gchp', m, eye).reshape(D_GROUP, S5_LANES)
    bmat = jnp.concatenate([expand_b(bbr), expand_b(bbi)], axis=1)
    expand_c = lambda m: jnp.einsum('gcp,gh->hpgc', m, eye).reshape(S5_LANES, D_GROUP)
    cmat = jnp.concatenate([expand_c(c_re), -expand_c(c_im)], axis=0)
    a_bar = jnp.stack([jnp.broadcast_to(ar.reshape(1, S5_LANES), (bsz, S5_LANES)),
                       jnp.broadcast_to(ai.reshape(1, S5_LANES), (bsz, S5_LANES))])
    return bmat.astype(BF16), cmat.astype(BF16), a_bar


def _gated_norm(o, gate, gain, e):
    y = o * lax.rsqrt(_seg_sum(o * o, e) * (1.0 / HEAD_DIM) + EPS)
    if gain is not None:
        y = y * gain
    return y * _silu(gate)


def _gelu_tanh(x):
    return 0.5 * x * (1.0 + jnp.tanh(np.sqrt(2.0 / np.pi) * (x + 0.044715 * (x * x * x))))


def _out_mlp_kernel(x_ref, mod_ref, haf_ref, hab_ref, hg_ref, rf_ref, rb_ref, rg_ref, gf_ref, gb_ref, gg_ref,
                    sy_ref, su_ref, hng_ref, gng_ref, sd_ref, glw_ref, glb_ref, wo_ref, n2_ref, w1_ref, w2_ref,
                    fg_ref, o_ref, *, final):
    m = mod_ref[0, 0]
    e = _head_eq(D_GROUP).astype(BF16)
    a = _gated_norm(haf_ref[0] + hab_ref[0], hg_ref[0], hng_ref[...], e)
    r = _gated_norm(rf_ref[0] + rb_ref[0], rg_ref[0], None, e)
    g = _gated_norm(gf_ref[0] + gb_ref[0], gg_ref[0], gng_ref[...], e)
    s = _gelu_tanh(sy_ref[0] + su_ref[0] * sd_ref[...])
    s = s * jax.nn.sigmoid(_dot(s, glw_ref[...]) + glb_ref[...])
    y = jnp.zeros((x_ref.shape[1], D_MODEL), F32)
    for i, part in enumerate((a, r, g, s)):
        y = y + _dot(part.astype(BF16), wo_ref[i * D_GROUP:(i + 1) * D_GROUP, :])
    x1 = x_ref[0] + m[2:3] * y
    h2 = _rms_mod(x1, n2_ref[...], m[3:4], m[4:5]).astype(BF16)
    acc = jnp.zeros_like(x1)
    for c0 in range(0, D_FF, FF_CHUNK):
        hid = jnp.maximum(_dot(h2, w1_ref[:, c0:c0 + FF_CHUNK]), 0.0)
        acc = acc + _dot((hid * hid).astype(BF16), w2_ref[c0:c0 + FF_CHUNK, :])
    x2 = x1 + m[5:6] * acc
    if final:
        x2 = x2 * lax.rsqrt(jnp.mean(x2 * x2, axis=-1, keepdims=True) + EPS) * fg_ref[...]
    o_ref[0] = x2


def _out_mlp(xc, modv, za, zr, zg, zs, ha, ra, ga, sy, hng, gng, sd, glw, glb, wo, n2, w1, w2, fg, nbc, final):
    b, l, d = xc.shape
    nb = l // BLK
    off = nbc if final else 0
    rows = lambda width, c=0: pl.BlockSpec((1, BLK, width), lambda i, j: (i, j + off, c))
    const = lambda arr: pl.BlockSpec(arr.shape, lambda i, j: (0,) * arr.ndim)
    seg = lambda i, j: (i, jnp.where(j + off < nbc, 0, 1), 0, 0)
    args = [xc, modv, ha[0], ha[1], za, ra[0], ra[1], zr, ga[0], ga[1], zg, sy, zs,
            hng, gng, sd, glw, glb, wo, n2, w1, w2, fg]
    in_specs = [rows(d), pl.BlockSpec((1, 1, 8, d), seg),
                rows(D_GROUP), rows(D_GROUP), rows(D_GROUP, 2),
                rows(D_GROUP), rows(D_GROUP), rows(D_GROUP, 3),
                rows(D_GROUP), rows(D_GROUP), rows(D_GROUP, 3),
                rows(D_GROUP), rows(D_GROUP)] + [const(a) for a in args[13:]]
    return pl.pallas_call(
        functools.partial(_out_mlp_kernel, final=final),
        grid=(b, nb - off),
        in_specs=in_specs,
        out_specs=pl.BlockSpec((1, BLK, d), lambda i, j: (i, j, 0)),
        out_shape=jax.ShapeDtypeStruct((b, l - off * BLK, d), F32),
        compiler_params=_cparams(("arbitrary", "arbitrary")),
        name="out_mlp_final" if final else "out_mlp",
    )(*args)


def kernel(x, c, ctx, c_ctx, mod_w, mod_b, norm1_g, norm2_g, w_in, hgrn_lb_logits, hgrn_norm_g, ret_decay_logit,
           gdn_conv_w, gdn_a_log, gdn_dt_bias, gdn_norm_g, s5_lam_re, s5_lam_im, s5_log_dt, s5_b_re, s5_b_im,
           s5_c_re, s5_c_im, s5_d, s5_glu_w, s5_glu_b, w_out, mlp_w1, mlp_w2, final_norm_g):
    bsz, t_lat, d = x.shape
    t_ctx = ctx.shape[1]
    depth = mod_w.shape[0]
    assert d == D_MODEL and t_ctx % BLK == 0 and t_lat % BLK == 0 and t_lat % GRID_W == 0
    assert t_ctx % S5_BLK == 0 and t_lat % S5_BLK == 0
    l = t_ctx + t_lat
    nbc = t_ctx // BLK

    xc = jnp.concatenate([ctx, x], axis=1).astype(F32)

    n_rows = -(-(bsz + 1) // 8) * 8
    cvecs = jnp.zeros((n_rows, d), F32).at[:bsz].set(c.astype(F32)).at[bsz].set(c_ctx.astype(F32))
    mod = _mod_proj(cvecs, mod_w.astype(F32), mod_b.astype(F32)).reshape(depth, n_rows, N_MOD, d)
    mod_lat = mod[:, :bsz]
    mod_ctx = jnp.broadcast_to(mod[:, bsz:bsz + 1], mod_lat.shape)
    modv = jnp.stack([mod_ctx, mod_lat], axis=2)
    modv = jnp.pad(modv, ((0, 0), (0, 0), (0, 0), (0, 8 - N_MOD), (0, 0)))

    sm = jax.nn.softmax(hgrn_lb_logits.astype(F32), axis=0)
    lbs = jnp.cumsum(sm, axis=0) - sm[:1]
    pos = jnp.arange(l, dtype=F32)
    half = HEAD_DIM // 2
    inv = ROPE_BASE ** (-jnp.arange(half, dtype=F32) / half)
    ang = pos[:, None] * inv[None, :]
    cos_t = jnp.tile(jnp.cos(ang), (1, 2 * N_HEADS))
    sin_t = jnp.tile(jnp.concatenate([-jnp.sin(ang), jnp.sin(ang)], axis=1), (1, N_HEADS))
    w_pad = jnp.concatenate([w_in[..., :ZA_W + ZR_W + ZG_W - 128 + 16].astype(F32),
                             jnp.zeros((depth, d, 112), F32),
                             w_in[..., ZA_W + ZR_W + ZG_W - 128 + 16:].astype(F32)], axis=-1).astype(BF16)
    pad8 = lambda v: jnp.pad(v.astype(F32).reshape(1, 2 * N_HEADS), ((0, 0), (0, 128 - 2 * N_HEADS)))

    out = None
    for li in range(depth):
        final = li == depth - 1
        za, zr, zg, zs = _in_proj(xc, modv[li], norm1_g[li].astype(F32).reshape(1, d), w_pad[li], nbc)
        ha = _hgrn_scan(za, lbs[li], nbc)
        lg = jax.nn.log_sigmoid(ret_decay_logit[li].astype(F32))
        ra = _ret_scan(zr, cos_t, sin_t, lg, nbc)
        qkv, gab = _gdn_prep(zg, gdn_conv_w[li].astype(F32).reshape(9, 3 * D_GROUP),
                             pad8(-jnp.exp(gdn_a_log[li].astype(F32))), pad8(gdn_dt_bias[li]), nbc)
        ga = _gdn_scan(qkv, gab, nbc)
        u_tb = jnp.transpose(zs, (1, 0, 2)).reshape(l * bsz, D_GROUP)
        nbc_s = t_ctx // S5_BLK
        prm = [s5_b_re[li].astype(F32), s5_b_im[li].astype(F32), s5_c_re[li].astype(F32), s5_c_im[li].astype(F32)]
        dis = [_s5_discretize(s5_lam_re[li, dd].astype(F32), s5_lam_im[li, dd].astype(F32),
                              s5_log_dt[li, dd].astype(F32), *prm, bsz) for dd in range(2)]
        y_b = _s5_scan(u_tb, *dis[1], bsz, nbc_s, True)
        y_tb = _s5_scan(u_tb, *dis[0], bsz, nbc_s, False, addend=y_b)
        sy = jnp.transpose(y_tb.reshape(l, bsz, D_GROUP), (1, 0, 2))
        row = lambda v: v.astype(F32).reshape(1, -1)
        res = _out_mlp(xc, modv[li], za, zr, zg, zs, ha, ra, ga, sy,
                       row(jnp.tile(hgrn_norm_g[li], N_HEADS)), row(jnp.tile(gdn_norm_g[li], N_HEADS)),
                       row(s5_d[li]), s5_glu_w[li].astype(F32), row(s5_glu_b[li]),
                       w_out[li].astype(BF16), row(norm2_g[li]), mlp_w1[li].astype(BF16), mlp_w2[li].astype(BF16),
                       row(final_norm_g), nbc, final)
        if final:
            out = res
        else:
            xc = res
    return out.astype(x.dtype)
```

```python
import functools

import jax
import jax.numpy as jnp
import numpy as np
from jax import lax
from jax.experimental import pallas as pl
from jax.experimental.pallas import tpu as pltpu

F32 = jnp.float32
BF16 = jnp.bfloat16

D_MODEL = 1024
D_GROUP = D_MODEL // 4
N_HEADS = 4
HEAD_DIM = D_GROUP // N_HEADS
D_FF = 4 * D_MODEL
N_MOD = 6
EPS = 1e-6
LB_FLOOR = 1e-30
LOG2E = 1.4426950408889634
GRID_W = 64
ROPE_BASE = 10000.0
S5_GROUP = 16
S5_GROUPS = D_GROUP // S5_GROUP
S5_STATE = 64
S5_LANES = S5_GROUPS * S5_STATE

BLK = 256
HG_CHUNK = 16
GDN_CHUNK = 64
GDN_SUB = 16
S5_BLK = 128
FF_CHUNK = 1024
V7X_VMEM_LIMIT = 56 * 1024 * 1024

ZA_W = 5 * D_GROUP
ZR_W = 4 * D_GROUP
ZG_W = 4 * D_GROUP + 128
ZS_W = D_GROUP
Z_W = ZA_W + ZR_W + ZG_W + ZS_W


def _cparams(sem):
    return pltpu.CompilerParams(dimension_semantics=sem, vmem_limit_bytes=V7X_VMEM_LIMIT)


def _bwd_block(s, nbc, nb):
    return jnp.where(s < nbc, nbc - 1 - s, nb - 1 - (s - nbc))


def _dot(a, b):
    return jnp.dot(a, b, preferred_element_type=F32)


def _dot_nt(a, b):
    return lax.dot_general(a, b, (((1,), (1,)), ((), ())), preferred_element_type=F32)


def _dot_tn(a, b):
    return lax.dot_general(a, b, (((0,), (0,)), ((), ())), preferred_element_type=F32)


def _bmm(a, b):
    return jnp.einsum('gij,gjk->gik', a, b, preferred_element_type=F32)


def _bmm_nt(a, b):
    return jnp.einsum('gik,gjk->gij', a, b, preferred_element_type=F32)


def _silu(x):
    return x * jax.nn.sigmoid(x)


def _head_eq(n):
    r = lax.shift_right_logical(lax.broadcasted_iota(jnp.int32, (n, n), 0), 6)
    c = lax.shift_right_logical(lax.broadcasted_iota(jnp.int32, (n, n), 1), 6)
    return r == c


def _seg_sum(x, e):
    hi = x.astype(BF16)
    lo = (x - hi.astype(F32)).astype(BF16)
    return _dot(hi, e) + _dot(lo, e)


def _rms_mod(x, g, shift, scale):
    h = x * lax.rsqrt(jnp.mean(x * x, axis=-1, keepdims=True) + EPS) * g
    return h * (1.0 + scale) + shift


def _chunk_cumsum(x, pos, c, rev):
    n = x.shape[0]
    sh = 1
    while sh < c:
        if rev:
            x = x + jnp.where(pos < c - sh, pltpu.roll(x, n - sh, 0), 0.0)
        else:
            x = x + jnp.where(pos >= sh, pltpu.roll(x, sh, 0), 0.0)
        sh *= 2
    return x


def _chunk_total(x, pos, c):
    n = x.shape[0]
    sh = 1
    while sh < c:
        x = x + jnp.where(pos >= sh, pltpu.roll(x, sh, 0), pltpu.roll(x, n - (c - sh), 0))
        sh *= 2
    return x


def _mod_kernel(c_ref, w_ref, b_ref, o_ref):
    o_ref[0] = _dot(_silu(c_ref[...]), w_ref[0]) + b_ref[0]


def _mod_proj(cvecs, mod_w, mod_b):
    depth, d, n = mod_w.shape
    rows = cvecs.shape[0]
    tn = 1536
    return pl.pallas_call(
        _mod_kernel,
        grid=(depth, n // tn),
        in_specs=[pl.BlockSpec((rows, d), lambda l, j: (0, 0)),
                  pl.BlockSpec((1, d, tn), lambda l, j: (l, 0, j)),
                  pl.BlockSpec((1, 1, tn), lambda l, j: (l, 0, j))],
        out_specs=pl.BlockSpec((1, rows, tn), lambda l, j: (l, 0, j)),
        out_shape=jax.ShapeDtypeStruct((depth, rows, n), F32),
        compiler_params=_cparams(("arbitrary", "arbitrary")),
        name="mod_proj",
    )(cvecs, mod_w, mod_b.reshape(depth, 1, n))


def _in_proj_kernel(x_ref, mod_ref, g_ref, w_ref, za_ref, zr_ref, zg_ref, zs_ref):
    m = mod_ref[0, 0]
    h = _rms_mod(x_ref[0], g_ref[...], m[0:1], m[1:2]).astype(BF16)
    o = 0
    for ref, width in ((za_ref, ZA_W), (zr_ref, ZR_W), (zg_ref, ZG_W), (zs_ref, ZS_W)):
        ref[0] = _dot(h, w_ref[:, o:o + width])
        o += width


def _in_proj(xc, modv, g, w, nbc):
    b, l, d = xc.shape
    nb = l // BLK
    row = lambda width: pl.BlockSpec((1, BLK, width), lambda i, j: (i, j, 0))
    return pl.pallas_call(
        _in_proj_kernel,
        grid=(b, nb),
        in_specs=[row(d),
                  pl.BlockSpec((1, 1, 8, d), lambda i, j: (i, jnp.where(j < nbc, 0, 1), 0, 0)),
                  pl.BlockSpec((1, d), lambda i, j: (0, 0)),
                  pl.BlockSpec((d, Z_W), lambda i, j: (0, 0))],
        out_specs=[row(ZA_W), row(ZR_W), row(ZG_W), row(ZS_W)],
        out_shape=[jax.ShapeDtypeStruct((b, l, width), F32) for width in (ZA_W, ZR_W, ZG_W, ZS_W)],
        compiler_params=_cparams(("arbitrary", "arbitrary")),
        name="in_proj",
    )(xc, modv, g, w)


def _hgrn_dir(q_raw, v, f_raw, lb, st, rev):
    n = q_raw.shape[0]
    c = HG_CHUNK
    pos = lax.broadcasted_iota(jnp.int32, (n, D_GROUP), 0) & (c - 1)
    heq = _head_eq(D_GROUP)
    e = heq.astype(BF16)
    qs = _silu(q_raw) * HEAD_DIM ** -0.5
    lbm = jnp.maximum(lb, LB_FLOOR)
    ex = jnp.exp(-jnp.abs(f_raw))
    inv = 1.0 / (1.0 + ex)
    sig_pos = jnp.where(f_raw >= 0, inv, ex * inv)
    sig_neg = jnp.where(f_raw >= 0, ex * inv, inv)
    logf = jnp.log(lbm * sig_neg + sig_pos) * LOG2E
    kk = (1.0 - lbm) * sig_neg
    bcum = _chunk_cumsum(logf, pos, c, rev)
    tot = _chunk_total(logf, pos, c)
    lk = jnp.log(kk) * LOG2E
    bk = bcum - lk

    nc, hc = n // c, c // 2
    split = lambda x: x.reshape(nc, 2, hc, D_GROUP)
    half = lambda x4, h: x4[:, h].reshape(nc * hc, D_GROUP)
    b4, bk4, lk4, v4, q4 = split(bcum), split(bk), split(lk), split(v), split(qs)
    bh = [half(b4, 0), half(b4, 1)]
    qh = [half(q4, 0), half(q4, 1)]
    pos8 = lax.broadcasted_iota(jnp.int32, (nc * hc, D_GROUP), 0) & (hc - 1)
    acc = [jnp.zeros((nc * hc, D_GROUP), F32), jnp.zeros((nc * hc, D_GROUP), F32)]
    qd = qs * jnp.exp2(bcum)
    kd = (kk * jnp.exp2(tot - bcum)).astype(BF16)
    vt = v.T.astype(BF16)
    cdec = jnp.exp2(tot)
    zeros = lambda r: jnp.zeros((r, D_GROUP), BF16)
    outs = [None] * nc
    for j in range(c):
        hj, jj = j // hc, j % hc
        row = lambda x4: jnp.broadcast_to(x4[:, hj, jj:jj + 1, :], (nc, hc, D_GROUP)).reshape(nc * hc, D_GROUP)
        bkj, lkj, vj = row(bk4), row(lk4), row(v4)
        other = 0 if rev else 1
        for ht in ((hj,) if hj == other else (hj, other)):
            term = qh[ht] * jnp.exp2(jnp.minimum(bh[ht] - bkj, lkj))
            contrib = _dot(term.astype(BF16), e) * vj
            if ht == hj:
                contrib = jnp.where((pos8 <= jj) if rev else (pos8 >= jj), contrib, 0.0)
            acc[ht] = acc[ht] + contrib

        ci = nc - 1 - j if rev else j
        rows = slice(ci * c, (ci + 1) * c)
        outs[ci] = _dot_nt(qd[rows], st)
        pieces = [zeros(ci * c), kd[rows], zeros(n - (ci + 1) * c)]
        kd_n = jnp.concatenate([p for p in pieces if p.shape[0]], axis=0)
        st = st * cdec[ci * c:ci * c + 1, :] + jnp.where(heq, _dot(vt, kd_n), 0.0)
    o = jnp.concatenate([acc[0].reshape(nc, 1, hc, D_GROUP), acc[1].reshape(nc, 1, hc, D_GROUP)],
                        axis=1).reshape(n, D_GROUP)
    return o + jnp.concatenate(outs, axis=0), st


def _hgrn_kernel(qf_ref, vf_ref, ff_ref, qb_ref, vb_ref, fb_ref, lb_ref, of_ref, ob_ref, st_ref):
    @pl.when(pl.program_id(1) == 0)
    def _():
        st_ref[...] = jnp.zeros_like(st_ref)

    o, st = _hgrn_dir(qf_ref[0], vf_ref[0], ff_ref[0], lb_ref[0:1, :], st_ref[0], False)
    of_ref[0] = o
    st_ref[0] = st
    o, st = _hgrn_dir(qb_ref[0], vb_ref[0], fb_ref[0], lb_ref[1:2, :], st_ref[1], True)
    ob_ref[0] = o
    st_ref[1] = st


def _scan_specs(nbc, nb, cols_f, cols_b, width=D_GROUP):
    fwd = [pl.BlockSpec((1, BLK, width), functools.partial(lambda i, s, c: (i, s, c), c=c)) for c in cols_f]
    bwd = [pl.BlockSpec((1, BLK, width), functools.partial(lambda i, s, c: (i, _bwd_block(s, nbc, nb), c), c=c))
           for c in cols_b]
    return fwd, bwd


def _hgrn_scan(za, lb, nbc):
    b, l, _ = za.shape
    nb = l // BLK
    fwd, bwd = _scan_specs(nbc, nb, (0, 1, 3), (0, 1, 4))
    outs = _scan_specs(nbc, nb, (0,), (0,))
    return pl.pallas_call(
        _hgrn_kernel,
        grid=(b, nb),
        in_specs=fwd + bwd + [pl.BlockSpec((2, D_GROUP), lambda i, s: (0, 0))],
        out_specs=outs[0] + outs[1],
        out_shape=[jax.ShapeDtypeStruct((b, l, D_GROUP), F32)] * 2,
        scratch_shapes=[pltpu.VMEM((2, D_GROUP, D_GROUP), F32)],
        compiler_params=_cparams(("arbitrary", "arbitrary")),
        name="hgrn_scan",
    )(za, za, za, za, za, za, lb)


def _rope(x, cos, sin_signed):
    lane = lax.broadcasted_iota(jnp.int32, x.shape, 1) & (HEAD_DIM - 1)
    half = HEAD_DIM // 2
    partner = jnp.where(lane < half, pltpu.roll(x, D_GROUP - half, 1), pltpu.roll(x, half, 1))
    return x * cos + partner * sin_signed


def _ret_dir(q, k, v, cos, sin, lg_ref, d, lg_lane, st, rev):
    n = q.shape[0]
    q = _rope(q, cos, sin)
    k = _rope(k, cos, sin) * HEAD_DIM ** -0.5
    ri = lax.broadcasted_iota(jnp.int32, (n, n), 0)
    ci = lax.broadcasted_iota(jnp.int32, (n, n), 1)
    rel = (ci - ri) if rev else (ri - ci)
    relf = jnp.maximum(rel, 0).astype(F32)
    t = lax.broadcasted_iota(jnp.int32, (n, D_GROUP), 0).astype(F32)
    lane_head = lax.shift_right_logical(lax.broadcasted_iota(jnp.int32, (n, D_GROUP), 1), 6)
    if rev:
        qdec, kdec = jnp.exp((n - t) * lg_lane), jnp.exp(t * lg_lane)
    else:
        qdec, kdec = jnp.exp((t + 1.0) * lg_lane), jnp.exp((n - 1.0 - t) * lg_lane)
    kb = k.astype(BF16)
    vb = v.astype(BF16)
    o = _dot((q * qdec).astype(BF16), st.astype(BF16))
    for h in range(N_HEADS):
        dmat = jnp.where(rel >= 0, jnp.exp(relf * lg_ref[d, h]), 0.0)
        s = _dot_nt(jnp.where(lane_head == h, q, 0.0).astype(BF16), kb)
        oh = _dot((s * dmat).astype(BF16), vb)
        o = o + jnp.where(lane_head == h, oh, 0.0)
    kv = _dot((k * kdec).T.astype(BF16), vb)
    st = st * jnp.exp(n * lg_lane) + jnp.where(_head_eq(D_GROUP), kv, 0.0)
    return o, st


def _ret_kernel(lg_ref, qf_ref, kf_ref, vf_ref, cf_ref, sf_ref, qb_ref, kb_ref, vb_ref, cb_ref, sb_ref,
                lgl_ref, of_ref, ob_ref, st_ref):
    @pl.when(pl.program_id(1) == 0)
    def _():
        st_ref[...] = jnp.zeros_like(st_ref)

    o, st = _ret_dir(qf_ref[0], kf_ref[0], vf_ref[0], cf_ref[...], sf_ref[...], lg_ref, 0,
                     lgl_ref[0:1, :], st_ref[0], False)
    of_ref[0] = o
    st_ref[0] = st
    o, st = _ret_dir(qb_ref[0], kb_ref[0], vb_ref[0], cb_ref[...], sb_ref[...], lg_ref, 1,
                     lgl_ref[1:2, :], st_ref[1], True)
    ob_ref[0] = o
    st_ref[1] = st


def _ret_scan(zr, cos_t, sin_t, lg, nbc):
    b, l, _ = zr.shape
    nb = l // BLK
    fwd, bwd = _scan_specs(nbc, nb, (0, 1, 2), (0, 1, 2))
    outs = _scan_specs(nbc, nb, (0,), (0,))
    tab_f = pl.BlockSpec((BLK, D_GROUP), lambda i, s: (s, 0))
    tab_b = pl.BlockSpec((BLK, D_GROUP), lambda i, s: (_bwd_block(s, nbc, nb), 0))
    lg_lane = jnp.repeat(lg, HEAD_DIM, axis=-1)
    return pl.pallas_call(
        _ret_kernel,
        grid=(b, nb),
        in_specs=[pl.BlockSpec(memory_space=pltpu.SMEM)] + fwd + [tab_f, tab_f] + bwd + [tab_b, tab_b]
                 + [pl.BlockSpec((2, D_GROUP), lambda i, s: (0, 0))],
        out_specs=outs[0] + outs[1],
        out_shape=[jax.ShapeDtypeStruct((b, l, D_GROUP), F32)] * 2,
        scratch_shapes=[pltpu.VMEM((2, D_GROUP, D_GROUP), F32)],
        compiler_params=_cparams(("arbitrary", "arbitrary")),
        name="ret_scan",
    )(lg, zr, zr, zr, cos_t, sin_t, zr, zr, zr, cos_t, sin_t, lg_lane)


def _gdn_prep_kernel(prev_ref, cur_ref, next_ref, ab_ref, cw_ref, an_ref, dtb_ref, qkv_ref, ga_ref, xs_ref,
                     *, nbc, nb):
    j = pl.program_id(1)
    is_ctx = j < nbc
    prev_ok = jnp.where(is_ctx, j > 0, j > nbc)
    next_ok = jnp.where(is_ctx, j < nbc - 1, j < nb - 1)
    n = BLK
    xs_ref[0:n, :] = jnp.where(prev_ok, prev_ref[0], 0.0)
    xs_ref[n:2 * n, :] = cur_ref[0]
    xs_ref[2 * n:3 * n, :] = jnp.where(next_ok, next_ref[0], 0.0)
    colpos = lax.broadcasted_iota(jnp.int32, (n, 1), 0) & (GRID_W - 1)
    acc = jnp.zeros((n, 3 * D_GROUP), F32)
    for dr in (-1, 0, 1):
        for dc in (-1, 0, 1):
            win = xs_ref[pl.ds(n + GRID_W * dr + dc, n), :]
            col_ok = (colpos >= 1) if dc == -1 else ((colpos <= GRID_W - 2) if dc == 1 else (colpos >= 0))
            ok = (is_ctx | col_ok) if dr == 0 else (jnp.logical_not(is_ctx) & col_ok)
            acc = acc + jnp.where(ok, win, 0.0) * cw_ref[(dr + 1) * 3 + (dc + 1):(dr + 1) * 3 + (dc + 2), :]
    xc = _silu(acc)
    e = _head_eq(D_GROUP).astype(BF16)
    q = xc[:, 0:D_GROUP]
    k = xc[:, D_GROUP:2 * D_GROUP]
    qkv_ref[0, :, 0:D_GROUP] = q * lax.rsqrt(_seg_sum(q * q, e) + EPS) * HEAD_DIM ** -0.5
    qkv_ref[0, :, D_GROUP:2 * D_GROUP] = k * lax.rsqrt(_seg_sum(k * k, e) + EPS)
    qkv_ref[0, :, 2 * D_GROUP:] = xc[:, 2 * D_GROUP:]
    ab = ab_ref[0]
    z = ab + dtb_ref[...]
    softplus = jnp.maximum(z, 0.0) + jnp.log(1.0 + jnp.exp(-jnp.abs(z)))
    lane = lax.broadcasted_iota(jnp.int32, ab.shape, 1)
    ga_ref[0] = jnp.where(lane < 2 * N_HEADS, an_ref[...] * softplus, jax.nn.sigmoid(ab))


def _gdn_prep(zg, conv_w, a_neg, dt_bias, nbc):
    b, l, _ = zg.shape
    nb = l // BLK
    w3 = 3 * D_GROUP
    return pl.pallas_call(
        functools.partial(_gdn_prep_kernel, nbc=nbc, nb=nb),
        grid=(b, nb),
        in_specs=[pl.BlockSpec((1, BLK, w3), lambda i, j: (i, jnp.maximum(j - 1, 0), 0)),
                  pl.BlockSpec((1, BLK, w3), lambda i, j: (i, j, 0)),
                  pl.BlockSpec((1, BLK, w3), lambda i, j: (i, jnp.minimum(j + 1, nb - 1), 0)),
                  pl.BlockSpec((1, BLK, 128), lambda i, j: (i, j, (ZG_W - 128) // 128)),
                  pl.BlockSpec((9, w3), lambda i, j: (0, 0)),
                  pl.BlockSpec((1, 128), lambda i, j: (0, 0)),
                  pl.BlockSpec((1, 128), lambda i, j: (0, 0))],
        out_specs=[pl.BlockSpec((1, BLK, w3), lambda i, j: (i, j, 0)),
                   pl.BlockSpec((1, BLK, 128), lambda i, j: (i, j, 0))],
        out_shape=[jax.ShapeDtypeStruct((b, l, w3), F32), jax.ShapeDtypeStruct((b, l, 128), F32)],
        scratch_shapes=[pltpu.VMEM((3 * BLK, w3), F32)],
        compiler_params=_cparams(("arbitrary", "arbitrary")),
        name="gdn_prep",
    )(zg, zg, zg, zg, conv_w, a_neg, dt_bias)


def _to_inst(x):
    nc = x.shape[0] // GDN_CHUNK
    parts = []
    for ci in range(nc):
        for h in range(N_HEADS):
            parts.append(x[ci * GDN_CHUNK:(ci + 1) * GDN_CHUNK, h * HEAD_DIM:(h + 1) * HEAD_DIM][None])
    return jnp.concatenate(parts, axis=0)


def _col_inst(x, lane0):
    nc = x.shape[0] // GDN_CHUNK
    parts = []
    for ci in range(nc):
        for h in range(N_HEADS):
            col = x[ci * GDN_CHUNK:(ci + 1) * GDN_CHUNK, lane0 + h:lane0 + h + 1]
            parts.append(jnp.broadcast_to(col, (GDN_CHUNK, HEAD_DIM))[None])
    return jnp.concatenate(parts, axis=0)


def _row_inst(xt, row0):
    nc = xt.shape[1] // GDN_CHUNK
    parts = []
    for ci in range(nc):
        for h in range(N_HEADS):
            row = xt[row0 + h:row0 + h + 1, ci * GDN_CHUNK:(ci + 1) * GDN_CHUNK]
            parts.append(jnp.broadcast_to(row, (HEAD_DIM, GDN_CHUNK))[None])
    return jnp.concatenate(parts, axis=0)


def _gdn_dir(qkv, ga, st, d, rev):
    n = qkv.shape[0]
    c = GDN_CHUNK
    nc = n // c
    pos = lax.broadcasted_iota(jnp.int32, ga.shape, 0) & (c - 1)
    g2 = _chunk_cumsum(ga, pos, c, rev)
    tot2 = _chunk_total(ga, pos, c)
    a0 = d * N_HEADS
    b0 = 2 * N_HEADS + d * N_HEADS
    gc = _col_inst(g2, a0)
    gr = _row_inst(g2.T, a0)
    totc = _col_inst(tot2, a0)
    beta = _col_inst(ga, b0)
    q3 = _to_inst(qkv[:, 0:D_GROUP])
    k3 = _to_inst(qkv[:, D_GROUP:2 * D_GROUP])
    v3 = _to_inst(qkv[:, 2 * D_GROUP:])

    ri = lax.broadcasted_iota(jnp.int32, (c, c), 0)
    ci = lax.broadcasted_iota(jnp.int32, (c, c), 1)
    tri = (ci >= ri) if rev else (ci <= ri)
    strict = (ci > ri) if rev else (ci < ri)
    same_sub = lax.shift_right_logical(ri, 4) == lax.shift_right_logical(ci, 4)
    eye = (ri == ci).astype(F32)

    lmat = jnp.where(tri, jnp.exp(jnp.minimum(gc - gr, 0.0)), 0.0)
    kb = k3 * beta
    amat = jnp.where(strict, _bmm_nt(kb, k3) * lmat, 0.0)
    dmat = jnp.where(same_sub, amat, 0.0)
    lo = amat - dmat
    d2 = _bmm(dmat, dmat)
    d4 = _bmm(d2, d2)
    d8 = _bmm(d4, d4)
    tdiag = _bmm(_bmm(eye - dmat, eye + d2), _bmm(eye + d4, eye + d8))
    mmat = _bmm(tdiag, lo)
    tinv = _bmm(_bmm(eye - mmat, eye + _bmm(mmat, mmat)), tdiag)
    u = _bmm(tinv, v3 * beta)
    w = _bmm(tinv, kb * jnp.exp(gc))
    qk = jnp.where(tri, _bmm_nt(q3, k3) * lmat, 0.0)
    qd = q3 * jnp.exp(gc)
    kd = k3 * jnp.exp(totc - gc)
    cdec = jnp.exp(totc)

    outs = [None] * nc
    order = range(nc - 1, -1, -1) if rev else range(nc)
    for cidx in order:
        sl = slice(cidx * N_HEADS, (cidx + 1) * N_HEADS)
        vnew = u[sl] - _bmm(w[sl], st)
        o = _bmm(qd[sl], st) + _bmm(qk[sl], vnew)
        st = st * cdec[sl] + jnp.einsum('gck,gcv->gkv', kd[sl], vnew, preferred_element_type=F32)
        outs[cidx] = jnp.concatenate([o[h] for h in range(N_HEADS)], axis=-1)
    return jnp.concatenate(outs, axis=0), st


def _gdn_kernel(xf_ref, gf_ref, xb_ref, gb_ref, of_ref, ob_ref, st_ref):
    @pl.when(pl.program_id(1) == 0)
    def _():
        st_ref[...] = jnp.zeros_like(st_ref)

    o, st = _gdn_dir(xf_ref[0], gf_ref[0], st_ref[0], 0, False)
    of_ref[0] = o
    st_ref[0] = st
    o, st = _gdn_dir(xb_ref[0], gb_ref[0], st_ref[1], 1, True)
    ob_ref[0] = o
    st_ref[1] = st


def _gdn_scan(qkv, ga, nbc):
    b, l, w3 = qkv.shape
    nb = l // BLK
    xf, xb = _scan_specs(nbc, nb, (0,), (0,), w3)
    gf, gb = _scan_specs(nbc, nb, (0,), (0,), 128)
    outs = _scan_specs(nbc, nb, (0,), (0,))
    return pl.pallas_call(
        _gdn_kernel,
        grid=(b, nb),
        in_specs=xf + gf + xb + gb,
        out_specs=outs[0] + outs[1],
        out_shape=[jax.ShapeDtypeStruct((b, l, D_GROUP), F32)] * 2,
        scratch_shapes=[pltpu.VMEM((2, N_HEADS, HEAD_DIM, HEAD_DIM), F32)],
        compiler_params=_cparams(("arbitrary", "arbitrary")),
        name="gdn_scan",
    )(qkv, ga, qkv, ga)


def _s5_kernel(u_ref, bm_ref, cm_ref, a_ref, *rest, rev, has_addend):
    if has_addend:
        add_ref, y_ref, x_ref, h_ref = rest
    else:
        y_ref, x_ref, h_ref = rest
    bsz = h_ref.shape[0]
    nt = u_ref.shape[0] // bsz

    @pl.when(pl.program_id(0) == 0)
    def _():
        h_ref[...] = jnp.zeros_like(h_ref)

    x_ref[...] = _dot(u_ref[...].astype(BF16), bm_ref[...])
    ar = a_ref[0]
    ai = a_ref[1]

    def step(i, carry):
        hr, hi = carry
        t = (nt - 1 - i) if rev else i
        rows = pl.ds(pl.multiple_of(t * bsz, bsz), bsz)
        nr = ar * hr - ai * hi + x_ref[rows, 0:S5_LANES]
        ni = ar * hi + ai * hr + x_ref[rows, S5_LANES:]
        x_ref[rows, 0:S5_LANES] = nr
        x_ref[rows, S5_LANES:] = ni
        return nr, ni

    hr, hi = lax.fori_loop(0, nt, step, (h_ref[:, 0:S5_LANES], h_ref[:, S5_LANES:]))
    h_ref[:, 0:S5_LANES] = hr
    h_ref[:, S5_LANES:] = hi
    y = _dot(x_ref[...].astype(BF16), cm_ref[...])
    y_ref[...] = y + add_ref[...] if has_addend else y


def _s5_scan(u_tb, bmat, cmat, a_bar, bsz, nbc_s, rev, addend=None):
    rows, w = u_tb.shape
    blk = S5_BLK * bsz
    nb = rows // blk
    idx = (lambda s: (_bwd_block(s, nbc_s, nb), 0)) if rev else (lambda s: (s, 0))
    const = lambda shape: pl.BlockSpec(shape, lambda s: (0,) * len(shape))
    in_specs = [pl.BlockSpec((blk, w), idx), const(bmat.shape), const(cmat.shape), const(a_bar.shape)]
    args = [u_tb, bmat, cmat, a_bar]
    if addend is not None:
        in_specs.append(pl.BlockSpec((blk, w), idx))
        args.append(addend)
    return pl.pallas_call(
        functools.partial(_s5_kernel, rev=rev, has_addend=addend is not None),
        grid=(nb,),
        in_specs=in_specs,
        out_specs=pl.BlockSpec((blk, w), idx),
        out_shape=jax.ShapeDtypeStruct((rows, w), F32),
        scratch_shapes=[pltpu.VMEM((blk, 2 * S5_LANES), F32), pltpu.VMEM((bsz, 2 * S5_LANES), F32)],
        compiler_params=_cparams(("arbitrary",)),
        name="s5_scan_bwd" if rev else "s5_scan_fwd",
    )(*args)


def _s5_discretize(lam_re, lam_im, log_dt, b_re, b_im, c_re, c_im, bsz):
    dt = jnp.exp(log_dt)[:, None]
    mag = jnp.exp(lam_re * dt)
    ar, ai = mag * jnp.cos(lam_im * dt), mag * jnp.sin(lam_im * dt)
    den = lam_re * lam_re + lam_im * lam_im
    nr, ni = ar - 1.0, ai
    fr = (nr * lam_re + ni * lam_im) / den
    fi = (ni * lam_re - nr * lam_im) / den
    bbr = fr[..., None] * b_re - fi[..., None] * b_im
    bbi = fr[..., None] * b_im + fi[..., None] * b_re
    eye = jnp.eye(S5_GROUPS, dtype=F32)
    expand_b = lambda m: jnp.einsum('gpc,gh->gchp', m, eye).reshape(D_GROUP, S5_LANES)
    bmat = jnp.concatenate([expand_b(bbr), expand_b(bbi)], axis=1)
    expand_c = lambda m: jnp.einsum('gcp,gh->hpgc', m, eye).reshape(S5_LANES, D_GROUP)
    cmat = jnp.concatenate([expand_c(c_re), -expand_c(c_im)], axis=0)
    a_bar = jnp.stack([jnp.broadcast_to(ar.reshape(1, S5_LANES), (bsz, S5_LANES)),
                       jnp.broadcast_to(ai.reshape(1, S5_LANES), (bsz, S5_LANES))])
    return bmat.astype(BF16), cmat.astype(BF16), a_bar


def _gated_norm(o, gate, gain, e):
    y = o * lax.rsqrt(_seg_sum(o * o, e) * (1.0 / HEAD_DIM) + EPS)
    if gain is not None:
        y = y * gain
    return y * _silu(gate)


def _gelu_tanh(x):
    return 0.5 * x * (1.0 + jnp.tanh(np.sqrt(2.0 / np.pi) * (x + 0.044715 * (x * x * x))))


def _out_mlp_kernel(x_ref, mod_ref, haf_ref, hab_ref, hg_ref, rf_ref, rb_ref, rg_ref, gf_ref, gb_ref, gg_ref,
                    sy_ref, su_ref, hng_ref, gng_ref, sd_ref, glw_ref, glb_ref, wo_ref, n2_ref, w1_ref, w2_ref,
                    fg_ref, o_ref, *, final):
    m = mod_ref[0, 0]
    e = _head_eq(D_GROUP).astype(BF16)
    a = _gated_norm(haf_ref[0] + hab_ref[0], hg_ref[0], hng_ref[...], e)
    r = _gated_norm(rf_ref[0] + rb_ref[0], rg_ref[0], None, e)
    g = _gated_norm(gf_ref[0] + gb_ref[0], gg_ref[0], gng_ref[...], e)
    s = _gelu_tanh(sy_ref[0] + su_ref[0] * sd_ref[...])
    s = s * jax.nn.sigmoid(_dot(s, glw_ref[...]) + glb_ref[...])
    y = jnp.zeros((x_ref.shape[1], D_MODEL), F32)
    for i, part in enumerate((a, r, g, s)):
        y = y + _dot(part.astype(BF16), wo_ref[i * D_GROUP:(i + 1) * D_GROUP, :])
    x1 = x_ref[0] + m[2:3] * y
    h2 = _rms_mod(x1, n2_ref[...], m[3:4], m[4:5]).astype(BF16)
    acc = jnp.zeros_like(x1)
    for c0 in range(0, D_FF, FF_CHUNK):
        hid = jnp.maximum(_dot(h2, w1_ref[:, c0:c0 + FF_CHUNK]), 0.0)
        acc = acc + _dot((hid * hid).astype(BF16), w2_ref[c0:c0 + FF_CHUNK, :])
    x2 = x1 + m[5:6] * acc
    if final:
        x2 = x2 * lax.rsqrt(jnp.mean(x2 * x2, axis=-1, keepdims=True) + EPS) * fg_ref[...]
    o_ref[0] = x2


def _out_mlp(xc, modv, za, zr, zg, zs, ha, ra, ga, sy, hng, gng, sd, glw, glb, wo, n2, w1, w2, fg, nbc, final):
    b, l, d = xc.shape
    nb = l // BLK
    off = nbc if final else 0
    rows = lambda width, c=0: pl.BlockSpec((1, BLK, width), lambda i, j: (i, j + off, c))
    const = lambda arr: pl.BlockSpec(arr.shape, lambda i, j: (0,) * arr.ndim)
    seg = lambda i, j: (i, jnp.where(j + off < nbc, 0, 1), 0, 0)
    args = [xc, modv, ha[0], ha[1], za, ra[0], ra[1], zr, ga[0], ga[1], zg, sy, zs,
            hng, gng, sd, glw, glb, wo, n2, w1, w2, fg]
    in_specs = [rows(d), pl.BlockSpec((1, 1, 8, d), seg),
                rows(D_GROUP), rows(D_GROUP), rows(D_GROUP, 2),
                rows(D_GROUP), rows(D_GROUP), rows(D_GROUP, 3),
                rows(D_GROUP), rows(D_GROUP), rows(D_GROUP, 3),
                rows(D_GROUP), rows(D_GROUP)] + [const(a) for a in args[13:]]
    return pl.pallas_call(
        functools.partial(_out_mlp_kernel, final=final),
        grid=(b, nb - off),
        in_specs=in_specs,
        out_specs=pl.BlockSpec((1, BLK, d), lambda i, j: (i, j, 0)),
        out_shape=jax.ShapeDtypeStruct((b, l - off * BLK, d), F32),
        compiler_params=_cparams(("arbitrary", "arbitrary")),
        name="out_mlp_final" if final else "out_mlp",
    )(*args)


def kernel(x, c, ctx, c_ctx, mod_w, mod_b, norm1_g, norm2_g, w_in, hgrn_lb_logits, hgrn_norm_g, ret_decay_logit,
           gdn_conv_w, gdn_a_log, gdn_dt_bias, gdn_norm_g, s5_lam_re, s5_lam_im, s5_log_dt, s5_b_re, s5_b_im,
           s5_c_re, s5_c_im, s5_d, s5_glu_w, s5_glu_b, w_out, mlp_w1, mlp_w2, final_norm_g):
    bsz, t_lat, d = x.shape
    t_ctx = ctx.shape[1]
    depth = mod_w.shape[0]
    assert d == D_MODEL and t_ctx % BLK == 0 and t_lat % BLK == 0 and t_lat % GRID_W == 0
    assert t_ctx % S5_BLK == 0 and t_lat % S5_BLK == 0
    l = t_ctx + t_lat
    nbc = t_ctx // BLK

    xc = jnp.concatenate([ctx, x], axis=1).astype(F32)

    n_rows = -(-(bsz + 1) // 8) * 8
    cvecs = jnp.zeros((n_rows, d), F32).at[:bsz].set(c.astype(F32)).at[bsz].set(c_ctx.astype(F32))
    mod = _mod_proj(cvecs, mod_w.astype(F32), mod_b.astype(F32)).reshape(depth, n_rows, N_MOD, d)
    mod_lat = mod[:, :bsz]
    mod_ctx = jnp.broadcast_to(mod[:, bsz:bsz + 1], mod_lat.shape)
    modv = jnp.stack([mod_ctx, mod_lat], axis=2)
    modv = jnp.pad(modv, ((0, 0), (0, 0), (0, 0), (0, 8 - N_MOD), (0, 0)))

    sm = jax.nn.softmax(hgrn_lb_logits.astype(F32), axis=0)
    lbs = jnp.cumsum(sm, axis=0) - sm[:1]
    pos = jnp.arange(l, dtype=F32)
    half = HEAD_DIM // 2
    inv = ROPE_BASE ** (-jnp.arange(half, dtype=F32) / half)
    ang = pos[:, None] * inv[None, :]
    cos_t = jnp.tile(jnp.cos(ang), (1, 2 * N_HEADS))
    sin_t = jnp.tile(jnp.concatenate([-jnp.sin(ang), jnp.sin(ang)], axis=1), (1, N_HEADS))
    w_pad = jnp.concatenate([w_in[..., :ZA_W + ZR_W + ZG_W - 128 + 16].astype(F32),
                             jnp.zeros((depth, d, 112), F32),
                             w_in[..., ZA_W + ZR_W + ZG_W - 128 + 16:].astype(F32)], axis=-1).astype(BF16)
    pad8 = lambda v: jnp.pad(v.astype(F32).reshape(1, 2 * N_HEADS), ((0, 0), (0, 128 - 2 * N_HEADS)))

    out = None
    for li in range(depth):
        final = li == depth - 1
        za, zr, zg, zs = _in_proj(xc, modv[li], norm1_g[li].astype(F32).reshape(1, d), w_pad[li], nbc)
        ha = _hgrn_scan(za, lbs[li], nbc)
        lg = jax.nn.log_sigmoid(ret_decay_logit[li].astype(F32))
        ra = _ret_scan(zr, cos_t, sin_t, lg, nbc)
        qkv, gab = _gdn_prep(zg, gdn_conv_w[li].astype(F32).reshape(9, 3 * D_GROUP),
                             pad8(-jnp.exp(gdn_a_log[li].astype(F32))), pad8(gdn_dt_bias[li]), nbc)
        ga = _gdn_scan(qkv, gab, nbc)
        u_tb = jnp.transpose(zs, (1, 0, 2)).reshape(l * bsz, D_GROUP)
        nbc_s = t_ctx // S5_BLK
        prm = [s5_b_re[li].astype(F32), s5_b_im[li].astype(F32), s5_c_re[li].astype(F32), s5_c_im[li].astype(F32)]
        dis = [_s5_discretize(s5_lam_re[li, dd].astype(F32), s5_lam_im[li, dd].astype(F32),
                              s5_log_dt[li, dd].astype(F32), *prm, bsz) for dd in range(2)]
        y_b = _s5_scan(u_tb, *dis[1], bsz, nbc_s, True)
        y_tb = _s5_scan(u_tb, *dis[0], bsz, nbc_s, False, addend=y_b)
        sy = jnp.transpose(y_tb.reshape(l, bsz, D_GROUP), (1, 0, 2))
        row = lambda v: v.astype(F32).reshape(1, -1)
        res = _out_mlp(xc, modv[li], za, zr, zg, zs, ha, ra, ga, sy,
                       row(jnp.tile(hgrn_norm_g[li], N_HEADS)), row(jnp.tile(gdn_norm_g[li], N_HEADS)),
                       row(s5_d[li]), s5_glu_w[li].astype(F32), row(s5_glu_b[li]),
                       w_out[li].astype(BF16), row(norm2_g[li]), mlp_w1[li].astype(BF16), mlp_w2[li].astype(BF16),
                       row(final_norm_g), nbc, final)
        if final:
            out = res
        else:
            xc = res
    return out.astype(x.dtype)
```

```python
import functools

import jax
import jax.numpy as jnp
import numpy as np
from jax import lax
from jax.experimental import pallas as pl
from jax.experimental.pallas import tpu as pltpu

F32 = jnp.float32
BF16 = jnp.bfloat16

D_MODEL = 1024
D_GROUP = D_MODEL // 4
N_HEADS = 4
HEAD_DIM = D_GROUP // N_HEADS
D_FF = 4 * D_MODEL
N_MOD = 6
EPS = 1e-6
LB_FLOOR = 1e-30
LOG2E = 1.4426950408889634
GRID_W = 64
ROPE_BASE = 10000.0
S5_GROUP = 16
S5_GROUPS = D_GROUP // S5_GROUP
S5_STATE = 64
S5_LANES = S5_GROUPS * S5_STATE

BLK = 256
HG_CHUNK = 16
GDN_CHUNK = 64
GDN_SUB = 16
S5_CHUNK = 4
S5_PART = 256
FF_CHUNK = 1024
V7X_VMEM_LIMIT = 56 * 1024 * 1024

ZA_W = 5 * D_GROUP
ZR_W = 4 * D_GROUP
ZG_W = 4 * D_GROUP + 128
ZS_W = D_GROUP
Z_W = ZA_W + ZR_W + ZG_W + ZS_W


def _cparams(sem):
    return pltpu.CompilerParams(dimension_semantics=sem, vmem_limit_bytes=V7X_VMEM_LIMIT)


def _bwd_block(s, nbc, nb):
    return jnp.where(s < nbc, nbc - 1 - s, nb - 1 - (s - nbc))


def _dot(a, b):
    return jnp.dot(a, b, preferred_element_type=F32)


def _dot_nt(a, b):
    return lax.dot_general(a, b, (((1,), (1,)), ((), ())), preferred_element_type=F32)


def _dot_tn(a, b):
    return lax.dot_general(a, b, (((0,), (0,)), ((), ())), preferred_element_type=F32)


def _bmm(a, b):
    return jnp.einsum('gij,gjk->gik', a, b, preferred_element_type=F32)


def _bmm_nt(a, b):
    return jnp.einsum('gik,gjk->gij', a, b, preferred_element_type=F32)


def _silu(x):
    return x * jax.nn.sigmoid(x)


def _head_eq(n):
    r = lax.shift_right_logical(lax.broadcasted_iota(jnp.int32, (n, n), 0), 6)
    c = lax.shift_right_logical(lax.broadcasted_iota(jnp.int32, (n, n), 1), 6)
    return r == c


def _seg_sum(x, e):
    hi = x.astype(BF16)
    lo = (x - hi.astype(F32)).astype(BF16)
    return _dot(hi, e) + _dot(lo, e)


def _rms_mod(x, g, shift, scale):
    h = x * lax.rsqrt(jnp.mean(x * x, axis=-1, keepdims=True) + EPS) * g
    return h * (1.0 + scale) + shift


def _chunk_cumsum(x, pos, c, rev):
    n = x.shape[0]
    sh = 1
    while sh < c:
        if rev:
            x = x + jnp.where(pos < c - sh, pltpu.roll(x, n - sh, 0), 0.0)
        else:
            x = x + jnp.where(pos >= sh, pltpu.roll(x, sh, 0), 0.0)
        sh *= 2
    return x


def _chunk_total(x, pos, c):
    n = x.shape[0]
    sh = 1
    while sh < c:
        x = x + jnp.where(pos >= sh, pltpu.roll(x, sh, 0), pltpu.roll(x, n - (c - sh), 0))
        sh *= 2
    return x


def _mod_kernel(c_ref, w_ref, b_ref, o_ref):
    o_ref[0] = _dot(_silu(c_ref[...]), w_ref[0]) + b_ref[0]


def _mod_proj(cvecs, mod_w, mod_b):
    depth, d, n = mod_w.shape
    rows = cvecs.shape[0]
    tn = 1536
    return pl.pallas_call(
        _mod_kernel,
        grid=(depth, n // tn),
        in_specs=[pl.BlockSpec((rows, d), lambda l, j: (0, 0)),
                  pl.BlockSpec((1, d, tn), lambda l, j: (l, 0, j)),
                  pl.BlockSpec((1, 1, tn), lambda l, j: (l, 0, j))],
        out_specs=pl.BlockSpec((1, rows, tn), lambda l, j: (l, 0, j)),
        out_shape=jax.ShapeDtypeStruct((depth, rows, n), F32),
        compiler_params=_cparams(("arbitrary", "arbitrary")),
        name="mod_proj",
    )(cvecs, mod_w, mod_b.reshape(depth, 1, n))


def _in_proj_kernel(x_ref, mod_ref, g_ref, w_ref, za_ref, zr_ref, zg_ref, zs_ref):
    m = mod_ref[0, 0]
    h = _rms_mod(x_ref[0], g_ref[...], m[0:1], m[1:2]).astype(BF16)
    o = 0
    for ref, width in ((za_ref, ZA_W), (zr_ref, ZR_W), (zg_ref, ZG_W), (zs_ref, ZS_W)):
        ref[0] = _dot(h, w_ref[:, o:o + width])
        o += width


def _in_proj(xc, modv, g, w, nbc):
    b, l, d = xc.shape
    nb = l // BLK
    row = lambda width: pl.BlockSpec((1, BLK, width), lambda i, j: (i, j, 0))
    return pl.pallas_call(
        _in_proj_kernel,
        grid=(b, nb),
        in_specs=[row(d),
                  pl.BlockSpec((1, 1, 8, d), lambda i, j: (i, jnp.where(j < nbc, 0, 1), 0, 0)),
                  pl.BlockSpec((1, d), lambda i, j: (0, 0)),
                  pl.BlockSpec((d, Z_W), lambda i, j: (0, 0))],
        out_specs=[row(ZA_W), row(ZR_W), row(ZG_W), row(ZS_W)],
        out_shape=[jax.ShapeDtypeStruct((b, l, width), F32) for width in (ZA_W, ZR_W, ZG_W, ZS_W)],
        compiler_params=_cparams(("arbitrary", "arbitrary")),
        name="in_proj",
    )(xc, modv, g, w)


def _hgrn_dir(q_raw, v, f_raw, lb, st, rev):
    n = q_raw.shape[0]
    c = HG_CHUNK
    pos = lax.broadcasted_iota(jnp.int32, (n, D_GROUP), 0) & (c - 1)
    heq = _head_eq(D_GROUP)
    e = heq.astype(BF16)
    qs = _silu(q_raw) * HEAD_DIM ** -0.5
    lbm = jnp.maximum(lb, LB_FLOOR)
    ex = jnp.exp(-jnp.abs(f_raw))
    inv = 1.0 / (1.0 + ex)
    sig_pos = jnp.where(f_raw >= 0, inv, ex * inv)
    sig_neg = jnp.where(f_raw >= 0, ex * inv, inv)
    logf = jnp.log(lbm * sig_neg + sig_pos) * LOG2E
    kk = (1.0 - lbm) * sig_neg
    bcum = _chunk_cumsum(logf, pos, c, rev)
    tot = _chunk_total(logf, pos, c)
    lk = jnp.log(kk) * LOG2E
    bk = bcum - lk

    nc, hc = n // c, c // 2
    split = lambda x: x.reshape(nc, 2, hc, D_GROUP)
    half = lambda x4, h: x4[:, h].reshape(nc * hc, D_GROUP)
    b4, bk4, lk4, v4, q4 = split(bcum), split(bk), split(lk), split(v), split(qs)
    bh = [half(b4, 0), half(b4, 1)]
    qh = [half(q4, 0), half(q4, 1)]
    pos8 = lax.broadcasted_iota(jnp.int32, (nc * hc, D_GROUP), 0) & (hc - 1)
    acc = [jnp.zeros((nc * hc, D_GROUP), F32), jnp.zeros((nc * hc, D_GROUP), F32)]
    qd = qs * jnp.exp2(bcum)
    kd = (kk * jnp.exp2(tot - bcum)).astype(BF16)
    vt = v.T.astype(BF16)
    cdec = jnp.exp2(tot)
    zeros = lambda r: jnp.zeros((r, D_GROUP), BF16)
    outs = [None] * nc
    for j in range(c):
        hj, jj = j // hc, j % hc
        row = lambda x4: jnp.broadcast_to(x4[:, hj, jj:jj + 1, :], (nc, hc, D_GROUP)).reshape(nc * hc, D_GROUP)
        bkj, lkj, vj = row(bk4), row(lk4), row(v4)
        other = 0 if rev else 1
        for ht in ((hj,) if hj == other else (hj, other)):
            term = qh[ht] * jnp.exp2(jnp.minimum(bh[ht] - bkj, lkj))
            contrib = _dot(term.astype(BF16), e) * vj
            if ht == hj:
                contrib = jnp.where((pos8 <= jj) if rev else (pos8 >= jj), contrib, 0.0)
            acc[ht] = acc[ht] + contrib

        ci = nc - 1 - j if rev else j
        rows = slice(ci * c, (ci + 1) * c)
        outs[ci] = _dot_nt(qd[rows], st)
        pieces = [zeros(ci * c), kd[rows], zeros(n - (ci + 1) * c)]
        kd_n = jnp.concatenate([p for p in pieces if p.shape[0]], axis=0)
        st = st * cdec[ci * c:ci * c + 1, :] + jnp.where(heq, _dot(vt, kd_n), 0.0)
    o = jnp.concatenate([acc[0].reshape(nc, 1, hc, D_GROUP), acc[1].reshape(nc, 1, hc, D_GROUP)],
                        axis=1).reshape(n, D_GROUP)
    return o + jnp.concatenate(outs, axis=0), st


def _hgrn_kernel(qf_ref, vf_ref, ff_ref, qb_ref, vb_ref, fb_ref, lb_ref, of_ref, ob_ref, st_ref):
    @pl.when(pl.program_id(1) == 0)
    def _():
        st_ref[...] = jnp.zeros_like(st_ref)

    o, st = _hgrn_dir(qf_ref[0], vf_ref[0], ff_ref[0], lb_ref[0:1, :], st_ref[0], False)
    of_ref[0] = o
    st_ref[0] = st
    o, st = _hgrn_dir(qb_ref[0], vb_ref[0], fb_ref[0], lb_ref[1:2, :], st_ref[1], True)
    ob_ref[0] = o
    st_ref[1] = st


def _scan_specs(nbc, nb, cols_f, cols_b, width=D_GROUP):
    fwd = [pl.BlockSpec((1, BLK, width), functools.partial(lambda i, s, c: (i, s, c), c=c)) for c in cols_f]
    bwd = [pl.BlockSpec((1, BLK, width), functools.partial(lambda i, s, c: (i, _bwd_block(s, nbc, nb), c), c=c))
           for c in cols_b]
    return fwd, bwd


def _hgrn_scan(za, lb, nbc):
    b, l, _ = za.shape
    nb = l // BLK
    fwd, bwd = _scan_specs(nbc, nb, (0, 1, 3), (0, 1, 4))
    outs = _scan_specs(nbc, nb, (0,), (0,))
    return pl.pallas_call(
        _hgrn_kernel,
        grid=(b, nb),
        in_specs=fwd + bwd + [pl.BlockSpec((2, D_GROUP), lambda i, s: (0, 0))],
        out_specs=outs[0] + outs[1],
        out_shape=[jax.ShapeDtypeStruct((b, l, D_GROUP), F32)] * 2,
        scratch_shapes=[pltpu.VMEM((2, D_GROUP, D_GROUP), F32)],
        compiler_params=_cparams(("arbitrary", "arbitrary")),
        name="hgrn_scan",
    )(za, za, za, za, za, za, lb)


def _rope(x, cos, sin_signed):
    lane = lax.broadcasted_iota(jnp.int32, x.shape, 1) & (HEAD_DIM - 1)
    half = HEAD_DIM // 2
    partner = jnp.where(lane < half, pltpu.roll(x, D_GROUP - half, 1), pltpu.roll(x, half, 1))
    return x * cos + partner * sin_signed


def _ret_dir(q, k, v, cos, sin, lg_ref, d, lg_lane, st, rev):
    n = q.shape[0]
    q = _rope(q, cos, sin)
    k = _rope(k, cos, sin) * HEAD_DIM ** -0.5
    ri = lax.broadcasted_iota(jnp.int32, (n, n), 0)
    ci = lax.broadcasted_iota(jnp.int32, (n, n), 1)
    rel = (ci - ri) if rev else (ri - ci)
    relf = jnp.maximum(rel, 0).astype(F32)
    t = lax.broadcasted_iota(jnp.int32, (n, D_GROUP), 0).astype(F32)
    lane_head = lax.shift_right_logical(lax.broadcasted_iota(jnp.int32, (n, D_GROUP), 1), 6)
    if rev:
        qdec, kdec = jnp.exp((n - t) * lg_lane), jnp.exp(t * lg_lane)
    else:
        qdec, kdec = jnp.exp((t + 1.0) * lg_lane), jnp.exp((n - 1.0 - t) * lg_lane)
    kb = k.astype(BF16)
    vb = v.astype(BF16)
    o = _dot((q * qdec).astype(BF16), st.astype(BF16))
    for h in range(N_HEADS):
        dmat = jnp.where(rel >= 0, jnp.exp(relf * lg_ref[d, h]), 0.0)
        s = _dot_nt(jnp.where(lane_head == h, q, 0.0).astype(BF16), kb)
        oh = _dot((s * dmat).astype(BF16), vb)
        o = o + jnp.where(lane_head == h, oh, 0.0)
    kv = _dot((k * kdec).T.astype(BF16), vb)
    st = st * jnp.exp(n * lg_lane) + jnp.where(_head_eq(D_GROUP), kv, 0.0)
    return o, st


def _ret_kernel(lg_ref, qf_ref, kf_ref, vf_ref, cf_ref, sf_ref, qb_ref, kb_ref, vb_ref, cb_ref, sb_ref,
                lgl_ref, of_ref, ob_ref, st_ref):
    @pl.when(pl.program_id(1) == 0)
    def _():
        st_ref[...] = jnp.zeros_like(st_ref)

    o, st = _ret_dir(qf_ref[0], kf_ref[0], vf_ref[0], cf_ref[...], sf_ref[...], lg_ref, 0,
                     lgl_ref[0:1, :], st_ref[0], False)
    of_ref[0] = o
    st_ref[0] = st
    o, st = _ret_dir(qb_ref[0], kb_ref[0], vb_ref[0], cb_ref[...], sb_ref[...], lg_ref, 1,
                     lgl_ref[1:2, :], st_ref[1], True)
    ob_ref[0] = o
    st_ref[1] = st


def _ret_scan(zr, cos_t, sin_t, lg, nbc):
    b, l, _ = zr.shape
    nb = l // BLK
    fwd, bwd = _scan_specs(nbc, nb, (0, 1, 2), (0, 1, 2))
    outs = _scan_specs(nbc, nb, (0,), (0,))
    tab_f = pl.BlockSpec((BLK, D_GROUP), lambda i, s: (s, 0))
    tab_b = pl.BlockSpec((BLK, D_GROUP), lambda i, s: (_bwd_block(s, nbc, nb), 0))
    lg_lane = jnp.repeat(lg, HEAD_DIM, axis=-1)
    return pl.pallas_call(
        _ret_kernel,
        grid=(b, nb),
        in_specs=[pl.BlockSpec(memory_space=pltpu.SMEM)] + fwd + [tab_f, tab_f] + bwd + [tab_b, tab_b]
                 + [pl.BlockSpec((2, D_GROUP), lambda i, s: (0, 0))],
        out_specs=outs[0] + outs[1],
        out_shape=[jax.ShapeDtypeStruct((b, l, D_GROUP), F32)] * 2,
        scratch_shapes=[pltpu.VMEM((2, D_GROUP, D_GROUP), F32)],
        compiler_params=_cparams(("arbitrary", "arbitrary")),
        name="ret_scan",
    )(lg, zr, zr, zr, cos_t, sin_t, zr, zr, zr, cos_t, sin_t, lg_lane)


def _gdn_prep_kernel(prev_ref, cur_ref, next_ref, ab_ref, cw_ref, an_ref, dtb_ref, qkv_ref, ga_ref, xs_ref,
                     *, nbc, nb):
    j = pl.program_id(1)
    is_ctx = j < nbc
    prev_ok = jnp.where(is_ctx, j > 0, j > nbc)
    next_ok = jnp.where(is_ctx, j < nbc - 1, j < nb - 1)
    n = BLK
    xs_ref[0:n, :] = jnp.where(prev_ok, prev_ref[0], 0.0)
    xs_ref[n:2 * n, :] = cur_ref[0]
    xs_ref[2 * n:3 * n, :] = jnp.where(next_ok, next_ref[0], 0.0)
    colpos = lax.broadcasted_iota(jnp.int32, (n, 1), 0) & (GRID_W - 1)
    acc = jnp.zeros((n, 3 * D_GROUP), F32)
    for dr in (-1, 0, 1):
        for dc in (-1, 0, 1):
            win = xs_ref[pl.ds(n + GRID_W * dr + dc, n), :]
            col_ok = (colpos >= 1) if dc == -1 else ((colpos <= GRID_W - 2) if dc == 1 else (colpos >= 0))
            ok = (is_ctx | col_ok) if dr == 0 else (jnp.logical_not(is_ctx) & col_ok)
            acc = acc + jnp.where(ok, win, 0.0) * cw_ref[(dr + 1) * 3 + (dc + 1):(dr + 1) * 3 + (dc + 2), :]
    xc = _silu(acc)
    e = _head_eq(D_GROUP).astype(BF16)
    q = xc[:, 0:D_GROUP]
    k = xc[:, D_GROUP:2 * D_GROUP]
    qkv_ref[0, :, 0:D_GROUP] = q * lax.rsqrt(_seg_sum(q * q, e) + EPS) * HEAD_DIM ** -0.5
    qkv_ref[0, :, D_GROUP:2 * D_GROUP] = k * lax.rsqrt(_seg_sum(k * k, e) + EPS)
    qkv_ref[0, :, 2 * D_GROUP:] = xc[:, 2 * D_GROUP:]
    ab = ab_ref[0]
    z = ab + dtb_ref[...]
    softplus = jnp.maximum(z, 0.0) + jnp.log(1.0 + jnp.exp(-jnp.abs(z)))
    lane = lax.broadcasted_iota(jnp.int32, ab.shape, 1)
    ga_ref[0] = jnp.where(lane < 2 * N_HEADS, an_ref[...] * softplus, jax.nn.sigmoid(ab))


def _gdn_prep(zg, conv_w, a_neg, dt_bias, nbc):
    b, l, _ = zg.shape
    nb = l // BLK
    w3 = 3 * D_GROUP
    return pl.pallas_call(
        functools.partial(_gdn_prep_kernel, nbc=nbc, nb=nb),
        grid=(b, nb),
        in_specs=[pl.BlockSpec((1, BLK, w3), lambda i, j: (i, jnp.maximum(j - 1, 0), 0)),
                  pl.BlockSpec((1, BLK, w3), lambda i, j: (i, j, 0)),
                  pl.BlockSpec((1, BLK, w3), lambda i, j: (i, jnp.minimum(j + 1, nb - 1), 0)),
                  pl.BlockSpec((1, BLK, 128), lambda i, j: (i, j, (ZG_W - 128) // 128)),
                  pl.BlockSpec((9, w3), lambda i, j: (0, 0)),
                  pl.BlockSpec((1, 128), lambda i, j: (0, 0)),
                  pl.BlockSpec((1, 128), lambda i, j: (0, 0))],
        out_specs=[pl.BlockSpec((1, BLK, w3), lambda i, j: (i, j, 0)),
                   pl.BlockSpec((1, BLK, 128), lambda i, j: (i, j, 0))],
        out_shape=[jax.ShapeDtypeStruct((b, l, w3), F32), jax.ShapeDtypeStruct((b, l, 128), F32)],
        scratch_shapes=[pltpu.VMEM((3 * BLK, w3), F32)],
        compiler_params=_cparams(("arbitrary", "arbitrary")),
        name="gdn_prep",
    )(zg, zg, zg, zg, conv_w, a_neg, dt_bias)


def _heads(x):
    return jnp.concatenate([x[:, h * HEAD_DIM:(h + 1) * HEAD_DIM][None] for h in range(N_HEADS)], axis=0)


def _col_heads(x, lane0):
    return jnp.concatenate([jnp.broadcast_to(x[:, lane0 + h:lane0 + h + 1], (GDN_CHUNK, HEAD_DIM))[None]
                            for h in range(N_HEADS)], axis=0)


def _row_heads(xt, row0):
    return jnp.concatenate([jnp.broadcast_to(xt[row0 + h:row0 + h + 1, :], (HEAD_DIM, GDN_CHUNK))[None]
                            for h in range(N_HEADS)], axis=0)


def _bmm16(a, b):
    return _bmm(a.astype(BF16), b.astype(BF16))


def _gdn_gates(ga, rev):
    pos = lax.broadcasted_iota(jnp.int32, ga.shape, 0) & (GDN_CHUNK - 1)
    g2 = _chunk_cumsum(ga, pos, GDN_CHUNK, rev)
    return g2, g2.T, _chunk_total(ga, pos, GDN_CHUNK)


def _gdn_load(group):
    qkv_ref, ga_ref, (g2, g2t, tot2), cidx, d = group
    c = GDN_CHUNK
    rows = slice(cidx * c, (cidx + 1) * c)
    a0 = d * N_HEADS
    b0 = 2 * N_HEADS + d * N_HEADS
    return dict(gc=_col_heads(g2[rows], a0), gr=_row_heads(g2t[:, rows], a0), totc=_col_heads(tot2[rows], a0),
                beta=_col_heads(ga_ref[0, rows, :], b0), q=_heads(qkv_ref[0, rows, 0:D_GROUP]),
                k=_heads(qkv_ref[0, rows, D_GROUP:2 * D_GROUP]), v=_heads(qkv_ref[0, rows, 2 * D_GROUP:]))


def _gdn_par(groups, n_fwd, out):
    c = GDN_CHUNK
    ops = [_gdn_load(g) for g in groups]
    cat = lambda name: jnp.concatenate([o[name] for o in ops], axis=0)
    gc, gr, totc, beta, q3, k3, v3 = (cat(nm) for nm in ('gc', 'gr', 'totc', 'beta', 'q', 'k', 'v'))
    n_inst = q3.shape[0]
    shape = (n_inst, c, c)
    rev = lax.broadcasted_iota(jnp.int32, shape, 0) >= n_fwd * N_HEADS
    ri = lax.broadcasted_iota(jnp.int32, shape, 1)
    ci = lax.broadcasted_iota(jnp.int32, shape, 2)
    lag = jnp.where(rev, ci - ri, ri - ci)
    tri = lag >= 0
    strict = lag > 0
    same_sub = lax.shift_right_logical(ri, 4) == lax.shift_right_logical(ci, 4)
    eye = (ri == ci).astype(F32)

    lmat = jnp.where(tri, jnp.exp(jnp.minimum(gc - gr, 0.0)), 0.0)
    kb = k3 * beta
    k16 = k3.astype(BF16)
    kkt = _bmm_nt(kb.astype(BF16), k16)
    qkt = _bmm_nt(q3.astype(BF16), k16)
    yield
    amat = jnp.where(strict, kkt * lmat, 0.0)
    dmat = jnp.where(same_sub, amat, 0.0)
    lo = amat - dmat
    d2 = _bmm16(dmat, dmat)
    yield
    d4 = _bmm16(d2, d2)
    p1 = _bmm16(eye - dmat, eye + d2)
    yield
    d8 = _bmm16(d4, d4)
    yield
    p2 = _bmm16(eye + d4, eye + d8)
    yield
    tdiag = _bmm16(p1, p2)
    yield
    mmat = _bmm16(tdiag, lo)
    yield
    m2 = _bmm16(mmat, mmat)
    yield
    p3 = _bmm16(eye - mmat, eye + m2)
    yield
    tinv = _bmm16(p3, tdiag).astype(BF16)
    yield
    out.update(u=_bmm(tinv, (v3 * beta).astype(BF16)),
               w=_bmm(tinv, (kb * jnp.exp(gc)).astype(BF16)).astype(BF16),
               qk=jnp.where(tri, qkt * lmat, 0.0).astype(BF16), qd=(q3 * jnp.exp(gc)).astype(BF16),
               kd=(k3 * jnp.exp(totc - gc)).astype(BF16), cdec=jnp.exp(totc))


def _gdn_seq(p, st_box, n_steps, write):
    nh = N_HEADS
    for k in range(n_steps):
        sel = lambda a: jnp.concatenate([a[nh * k:nh * (k + 1)], a[nh * (n_steps + k):nh * (n_steps + k + 1)]], axis=0)
        st = st_box[0]
        s16 = st.astype(BF16)
        ws = _bmm(sel(p['w']), s16)
        qs = _bmm(sel(p['qd']), s16)
        yield
        v16 = (sel(p['u']) - ws).astype(BF16)
        o = qs + _bmm(sel(p['qk']), v16)
        st_box[0] = st * sel(p['cdec']) + jnp.einsum('gck,gcv->gkv', sel(p['kd']), v16, preferred_element_type=F32)
        write(k, jnp.concatenate([o[h] for h in range(nh)], axis=-1),
              jnp.concatenate([o[nh + h] for h in range(nh)], axis=-1))
        yield


def _interleave(*gens):
    gens = list(gens)
    while gens:
        for g in list(gens):
            try:
                next(g)
            except StopIteration:
                gens.remove(g)


def _gdn_kernel(xf_ref, gf_ref, xb_ref, gb_ref, of_ref, ob_ref, st_ref):
    @pl.when(pl.program_id(1) == 0)
    def _():
        st_ref[...] = jnp.zeros_like(st_ref)

    c = GDN_CHUNK
    nc = xf_ref.shape[1] // c
    half = nc // 2
    gates_f = _gdn_gates(gf_ref[0], False)
    gates_b = _gdn_gates(gb_ref[0], True)
    fwd = lambda ci: (xf_ref, gf_ref, gates_f, ci, 0)
    bwd = lambda ci: (xb_ref, gb_ref, gates_b, ci, 1)

    def writer(first_f, first_b):
        def write(k, o_f, o_b):
            cf, cb = first_f + k, first_b - k
            of_ref[0, cf * c:(cf + 1) * c, :] = o_f
            ob_ref[0, cb * c:(cb + 1) * c, :] = o_b
        return write

    st_box = [jnp.concatenate([st_ref[0], st_ref[1]], axis=0)]
    p1, p2 = {}, {}
    _interleave(_gdn_par([fwd(t) for t in range(half)] + [bwd(nc - 1 - t) for t in range(half)], half, p1))
    _interleave(_gdn_par([fwd(half + t) for t in range(half)] + [bwd(nc - 1 - half - t) for t in range(half)],
                         half, p2),
                _gdn_seq(p1, st_box, half, writer(0, nc - 1)))
    _interleave(_gdn_seq(p2, st_box, half, writer(half, nc - 1 - half)))
    st_ref[0] = st_box[0][:N_HEADS]
    st_ref[1] = st_box[0][N_HEADS:]


def _gdn_scan(qkv, ga, nbc):
    b, l, w3 = qkv.shape
    nb = l // BLK
    xf, xb = _scan_specs(nbc, nb, (0,), (0,), w3)
    gf, gb = _scan_specs(nbc, nb, (0,), (0,), 128)
    outs = _scan_specs(nbc, nb, (0,), (0,))
    return pl.pallas_call(
        _gdn_kernel,
        grid=(b, nb),
        in_specs=xf + gf + xb + gb,
        out_specs=outs[0] + outs[1],
        out_shape=[jax.ShapeDtypeStruct((b, l, D_GROUP), F32)] * 2,
        scratch_shapes=[pltpu.VMEM((2, N_HEADS, HEAD_DIM, HEAD_DIM), F32)],
        compiler_params=_cparams(("arbitrary", "arbitrary")),
        name="gdn_scan",
    )(qkv, ga, qkv, ga)


def _cmul(ar, ai, hr, hi):
    return ar * hr - ai * hi, ar * hi + ai * hr


def _s5_pack(re, im):
    lead = re.shape[:-1]
    parts = jnp.stack([re.reshape(lead + (-1, S5_PART)), im.reshape(lead + (-1, S5_PART))], axis=-2)
    return parts.reshape(lead + (2 * S5_LANES,))


def _s5_kernel(u_ref, ws_ref, wc_ref, wt_ref, aux_ref, pw_ref, y_ref, h_ref, *, rev, parts):
    u = u_ref[0].astype(BF16)
    rows, width = h_ref.shape
    pw = S5_PART
    pieces = [(slice(o, o + pw), slice(o + pw, o + 2 * pw)) for o in range(0, width, 2 * pw)]
    pos = lax.broadcasted_iota(jnp.int32, (rows // 8, 8, pw), 1)
    gw = pw // S5_STATE * S5_GROUP
    u_q = [jnp.concatenate([u[:, i * D_GROUP + q * gw:i * D_GROUP + (q + 1) * gw] for i in range(S5_CHUNK)], axis=1)
           for q in range(len(pieces))]
    for q, (re, im) in enumerate(pieces):
        x = _dot(u_q[q], ws_ref[q])
        xr = x[:, 0:pw].reshape(rows // 8, 8, pw)
        xi = x[:, pw:].reshape(rows // 8, 8, pw)
        for lvl, k in enumerate((1, 2, 4)):
            dr, di = _cmul(aux_ref[lvl:lvl + 1, re], aux_ref[lvl:lvl + 1, im],
                           pltpu.roll(xr, 8 - k if rev else k, 1), pltpu.roll(xi, 8 - k if rev else k, 1))
            valid = (pos < 8 - k) if rev else (pos >= k)
            xr = xr + jnp.where(valid, dr, 0.0)
            xi = xi + jnp.where(valid, di, 0.0)
        h_ref[:, re] = xr.reshape(rows, pw)
        h_ref[:, im] = xi.reshape(rows, pw)

    sub = lax.broadcasted_iota(jnp.int32, (8, pw), 0)
    edge = 0 if rev else 7

    def tile_step(t, carry):
        sl = pl.ds(pl.multiple_of(t * 8, 8), 8)
        out = []
        for (re, im), (cr, ci) in zip(pieces, carry):
            dr, di = _cmul(pw_ref[:, re], pw_ref[:, im], cr, ci)
            fr, fi = h_ref[sl, re] + dr, h_ref[sl, im] + di
            first = sub == (7 if rev else 0)
            h_ref[sl, re] = jnp.where(first, cr, pltpu.roll(fr, 7 if rev else 1, 0))
            h_ref[sl, im] = jnp.where(first, ci, pltpu.roll(fi, 7 if rev else 1, 0))
            out.append((jnp.broadcast_to(fr[edge:edge + 1, :], fr.shape),
                        jnp.broadcast_to(fi[edge:edge + 1, :], fi.shape)))
        return tuple(out)

    carry = tuple((jnp.zeros((8, pw), F32), jnp.zeros((8, pw), F32)) for _ in pieces)
    for row0, nrows in parts:
        t0, nt = row0 // 8, nrows // 8
        if rev:
            carry = lax.fori_loop(0, nt, lambda i, c, t0=t0, nt=nt: tile_step(t0 + nt - 1 - i, c), carry)
        else:
            carry = lax.fori_loop(0, nt, lambda i, c, t0=t0: tile_step(t0 + i, c), carry)
    y_q = [_dot(u_q[q], wt_ref[q]) + _dot(h_ref[:, re.start:im.stop].astype(BF16), wc_ref[q])
           for q, (re, im) in enumerate(pieces)]
    y_ref[0] = jnp.concatenate([y_q[q][:, i * gw:(i + 1) * gw] for i in range(S5_CHUNK) for q in range(len(pieces))],
                               axis=1)


def _s5_scan(u_c, weights, parts, rev):
    bsz, rows, w = u_c.shape
    const = lambda a: pl.BlockSpec(a.shape, lambda b: (0,) * a.ndim, pipeline_mode=pl.Buffered(1))
    row_spec = pl.BlockSpec((1, rows, w), lambda b: (b, 0, 0))
    return pl.pallas_call(
        functools.partial(_s5_kernel, rev=rev, parts=parts),
        grid=(bsz,),
        in_specs=[row_spec] + [const(a) for a in weights],
        out_specs=row_spec,
        out_shape=jax.ShapeDtypeStruct((bsz, rows, w), F32),
        scratch_shapes=[pltpu.VMEM((rows, 2 * S5_LANES), F32)],
        compiler_params=_cparams(("arbitrary",)),
        name="s5_scan_bwd" if rev else "s5_scan_fwd",
    )(u_c, *weights)


def _s5_weights(lam_re, lam_im, log_dt, b_re, b_im, c_re, c_im, rev, parts):
    cch, ng = S5_CHUNK, S5_GROUPS
    dt = jnp.exp(log_dt)[:, None]
    ang, dec = lam_im * dt, lam_re * dt
    power = lambda n: (jnp.exp(n * dec) * jnp.cos(n * ang), jnp.exp(n * dec) * jnp.sin(n * ang))
    ar, ai = power(1.0)
    den = lam_re * lam_re + lam_im * lam_im
    nr, ni = ar - 1.0, ai
    fr = (nr * lam_re + ni * lam_im) / den
    fi = (ni * lam_re - nr * lam_im) / den
    bbr = fr[..., None] * b_re - fi[..., None] * b_im
    bbi = fr[..., None] * b_im + fi[..., None] * b_re
    pows = [power(float(n)) for n in range(cch + 1)]
    eye = jnp.eye(ng, dtype=F32)
    after = lambda i: i if rev else cch - 1 - i
    ws_r = jnp.stack([pows[after(i)][0][..., None] * bbr - pows[after(i)][1][..., None] * bbi for i in range(cch)])
    ws_i = jnp.stack([pows[after(i)][0][..., None] * bbi + pows[after(i)][1][..., None] * bbr for i in range(cch)])
    exp_s = lambda m: jnp.einsum('igpc,gh->igchp', m, eye).reshape(cch * D_GROUP, S5_LANES)
    ws = _s5_pack(exp_s(ws_r), exp_s(ws_i))
    upto = lambda i: cch - i if rev else i + 1
    ca_r = lambda n: c_re * pows[n][0][:, None, :] - c_im * pows[n][1][:, None, :]
    ca_i = lambda n: c_re * pows[n][1][:, None, :] + c_im * pows[n][0][:, None, :]
    exp_c = lambda f: jnp.einsum('igcp,gh->igchp', jnp.stack([f(upto(i)) for i in range(cch)]),
                                 eye).reshape(cch * D_GROUP, S5_LANES)
    wc = _s5_pack(exp_c(ca_r), -exp_c(ca_i)).T
    taps = [jnp.einsum('gcp,gpd->gdc', ca_r(n), bbr) - jnp.einsum('gcp,gpd->gdc', ca_i(n), bbi) for n in range(cch)]
    zero = jnp.zeros_like(taps[0])
    lagged = lambda j, i: (j - i) if rev else (i - j)
    blocks = jnp.stack([jnp.stack([taps[lagged(j, i)] if lagged(j, i) >= 0 else zero for i in range(cch)])
                        for j in range(cch)])
    wt = jnp.einsum('jigdc,gh->jgdihc', blocks, eye).reshape(cch * D_GROUP, cch * D_GROUP)
    flat = lambda re_im: _s5_pack(re_im[0].reshape(1, S5_LANES), re_im[1].reshape(1, S5_LANES))
    aux = jnp.concatenate([flat(power(float(cch * k))) for k in (1, 2, 4)], axis=0)
    aux = jnp.pad(aux, ((0, 8 - aux.shape[0]), (0, 0)))
    pw = jnp.concatenate([flat(power(float(cch * ((8 - p) if rev else (p + 1))))) for p in range(8)], axis=0)
    nq, gw = S5_LANES // S5_PART, S5_PART // S5_STATE * S5_GROUP
    ws6 = ws.reshape(cch, nq, gw, nq, 2 * S5_PART)
    wc5 = wc.reshape(nq, 2 * S5_PART, cch, nq, gw)
    wt6 = wt.reshape(cch, nq, gw, cch, nq, gw)
    ws = jnp.stack([ws6[:, q, :, q, :].reshape(cch * gw, 2 * S5_PART) for q in range(nq)])
    wc = jnp.stack([wc5[q, :, :, q, :].reshape(2 * S5_PART, cch * gw) for q in range(nq)])
    wt = jnp.stack([wt6[:, q, :, :, q, :].reshape(cch * gw, cch * gw) for q in range(nq)])
    return ws.astype(BF16), wc.astype(BF16), wt.astype(BF16), aux, pw


def _gated_norm(o, gate, gain, e):
    y = o * lax.rsqrt(_seg_sum(o * o, e) * (1.0 / HEAD_DIM) + EPS)
    if gain is not None:
        y = y * gain
    return y * _silu(gate)


def _gelu_tanh(x):
    return 0.5 * x * (1.0 + jnp.tanh(np.sqrt(2.0 / np.pi) * (x + 0.044715 * (x * x * x))))


def _out_mlp_kernel(x_ref, mod_ref, haf_ref, hab_ref, hg_ref, rf_ref, rb_ref, rg_ref, gf_ref, gb_ref, gg_ref,
                    sf_ref, sb_ref, su_ref, hng_ref, gng_ref, sd_ref, glw_ref, glb_ref, wo_ref, n2_ref, w1_ref,
                    w2_ref, fg_ref, o_ref, *, final):
    m = mod_ref[0, 0]
    e = _head_eq(D_GROUP).astype(BF16)
    a = _gated_norm(haf_ref[0] + hab_ref[0], hg_ref[0], hng_ref[...], e)
    r = _gated_norm(rf_ref[0] + rb_ref[0], rg_ref[0], None, e)
    g = _gated_norm(gf_ref[0] + gb_ref[0], gg_ref[0], gng_ref[...], e)
    s = _gelu_tanh(sf_ref[0] + sb_ref[0] + su_ref[0] * sd_ref[...])
    s = s * jax.nn.sigmoid(_dot(s, glw_ref[...]) + glb_ref[...])
    y = jnp.zeros((x_ref.shape[1], D_MODEL), F32)
    for i, part in enumerate((a, r, g, s)):
        y = y + _dot(part.astype(BF16), wo_ref[i * D_GROUP:(i + 1) * D_GROUP, :])
    x1 = x_ref[0] + m[2:3] * y
    h2 = _rms_mod(x1, n2_ref[...], m[3:4], m[4:5]).astype(BF16)
    acc = jnp.zeros_like(x1)
    for c0 in range(0, D_FF, FF_CHUNK):
        hid = jnp.maximum(_dot(h2, w1_ref[:, c0:c0 + FF_CHUNK]), 0.0)
        acc = acc + _dot((hid * hid).astype(BF16), w2_ref[c0:c0 + FF_CHUNK, :])
    x2 = x1 + m[5:6] * acc
    if final:
        x2 = x2 * lax.rsqrt(jnp.mean(x2 * x2, axis=-1, keepdims=True) + EPS) * fg_ref[...]
    o_ref[0] = x2


def _out_mlp(xc, modv, za, zr, zg, zs, ha, ra, ga, sa, hng, gng, sd, glw, glb, wo, n2, w1, w2, fg, nbc, final):
    b, l, d = xc.shape
    nb = l // BLK
    off = nbc if final else 0
    rows = lambda width, c=0: pl.BlockSpec((1, BLK, width), lambda i, j: (i, j + off, c))
    const = lambda arr: pl.BlockSpec(arr.shape, lambda i, j: (0,) * arr.ndim)
    seg = lambda i, j: (i, jnp.where(j + off < nbc, 0, 1), 0, 0)
    weights = [hng, gng, sd, glw, glb, wo, n2, w1, w2, fg]
    args = [xc, modv, ha[0], ha[1], za, ra[0], ra[1], zr, ga[0], ga[1], zg, sa[0], sa[1], zs] + weights
    in_specs = [rows(d), pl.BlockSpec((1, 1, 8, d), seg),
                rows(D_GROUP), rows(D_GROUP), rows(D_GROUP, 2),
                rows(D_GROUP), rows(D_GROUP), rows(D_GROUP, 3),
                rows(D_GROUP), rows(D_GROUP), rows(D_GROUP, 3),
                rows(D_GROUP), rows(D_GROUP), rows(D_GROUP)] + [const(a) for a in weights]
    return pl.pallas_call(
        functools.partial(_out_mlp_kernel, final=final),
        grid=(b, nb - off),
        in_specs=in_specs,
        out_specs=pl.BlockSpec((1, BLK, d), lambda i, j: (i, j, 0)),
        out_shape=jax.ShapeDtypeStruct((b, l - off * BLK, d), F32),
        compiler_params=_cparams(("arbitrary", "arbitrary")),
        name="out_mlp_final" if final else "out_mlp",
    )(*args)


def kernel(x, c, ctx, c_ctx, mod_w, mod_b, norm1_g, norm2_g, w_in, hgrn_lb_logits, hgrn_norm_g, ret_decay_logit,
           gdn_conv_w, gdn_a_log, gdn_dt_bias, gdn_norm_g, s5_lam_re, s5_lam_im, s5_log_dt, s5_b_re, s5_b_im,
           s5_c_re, s5_c_im, s5_d, s5_glu_w, s5_glu_b, w_out, mlp_w1, mlp_w2, final_norm_g):
    bsz, t_lat, d = x.shape
    t_ctx = ctx.shape[1]
    depth = mod_w.shape[0]
    assert d == D_MODEL and t_ctx % BLK == 0 and t_lat % BLK == 0 and t_lat % GRID_W == 0
    assert t_ctx % (8 * S5_CHUNK) == 0 and t_lat % (8 * S5_CHUNK) == 0
    l = t_ctx + t_lat
    nbc = t_ctx // BLK

    xc = jnp.concatenate([ctx, x], axis=1).astype(F32)

    n_rows = -(-(bsz + 1) // 8) * 8
    cvecs = jnp.zeros((n_rows, d), F32).at[:bsz].set(c.astype(F32)).at[bsz].set(c_ctx.astype(F32))
    mod = _mod_proj(cvecs, mod_w.astype(F32), mod_b.astype(F32)).reshape(depth, n_rows, N_MOD, d)
    mod_lat = mod[:, :bsz]
    mod_ctx = jnp.broadcast_to(mod[:, bsz:bsz + 1], mod_lat.shape)
    modv = jnp.stack([mod_ctx, mod_lat], axis=2)
    modv = jnp.pad(modv, ((0, 0), (0, 0), (0, 0), (0, 8 - N_MOD), (0, 0)))

    sm = jax.nn.softmax(hgrn_lb_logits.astype(F32), axis=0)
    lbs = jnp.cumsum(sm, axis=0) - sm[:1]
    pos = jnp.arange(l, dtype=F32)
    half = HEAD_DIM // 2
    inv = ROPE_BASE ** (-jnp.arange(half, dtype=F32) / half)
    ang = pos[:, None] * inv[None, :]
    cos_t = jnp.tile(jnp.cos(ang), (1, 2 * N_HEADS))
    sin_t = jnp.tile(jnp.concatenate([-jnp.sin(ang), jnp.sin(ang)], axis=1), (1, N_HEADS))
    w_pad = jnp.concatenate([w_in[..., :ZA_W + ZR_W + ZG_W - 128 + 16].astype(F32),
                             jnp.zeros((depth, d, 112), F32),
                             w_in[..., ZA_W + ZR_W + ZG_W - 128 + 16:].astype(F32)], axis=-1).astype(BF16)
    pad8 = lambda v: jnp.pad(v.astype(F32).reshape(1, 2 * N_HEADS), ((0, 0), (0, 128 - 2 * N_HEADS)))

    out = None
    for li in range(depth):
        final = li == depth - 1
        za, zr, zg, zs = _in_proj(xc, modv[li], norm1_g[li].astype(F32).reshape(1, d), w_pad[li], nbc)
        ha = _hgrn_scan(za, lbs[li], nbc)
        lg = jax.nn.log_sigmoid(ret_decay_logit[li].astype(F32))
        ra = _ret_scan(zr, cos_t, sin_t, lg, nbc)
        qkv, gab = _gdn_prep(zg, gdn_conv_w[li].astype(F32).reshape(9, 3 * D_GROUP),
                             pad8(-jnp.exp(gdn_a_log[li].astype(F32))), pad8(gdn_dt_bias[li]), nbc)
        ga = _gdn_scan(qkv, gab, nbc)
        u_c = zs.reshape(bsz, l // S5_CHUNK, S5_CHUNK * D_GROUP)
        parts = ((0, t_ctx // S5_CHUNK), (t_ctx // S5_CHUNK, t_lat // S5_CHUNK))
        prm = [s5_b_re[li].astype(F32), s5_b_im[li].astype(F32), s5_c_re[li].astype(F32), s5_c_im[li].astype(F32)]
        wts = [_s5_weights(s5_lam_re[li, dd].astype(F32), s5_lam_im[li, dd].astype(F32),
                           s5_log_dt[li, dd].astype(F32), *prm, dd == 1, parts) for dd in range(2)]
        sa = [_s5_scan(u_c, wts[dd], parts, dd == 1).reshape(bsz, l, D_GROUP) for dd in range(2)]
        row = lambda v: v.astype(F32).reshape(1, -1)
        res = _out_mlp(xc, modv[li], za, zr, zg, zs, ha, ra, ga, sa,
                       row(jnp.tile(hgrn_norm_g[li], N_HEADS)), row(jnp.tile(gdn_norm_g[li], N_HEADS)),
                       row(s5_d[li]), s5_glu_w[li].astype(F32), row(s5_glu_b[li]),
                       w_out[li].astype(BF16), row(norm2_g[li]), mlp_w1[li].astype(BF16), mlp_w2[li].astype(BF16),
                       row(final_norm_g), nbc, final)
        if final:
            out = res
        else:
            xc = res
    return out.astype(x.dtype)
```

```python
import functools

import jax
import jax.numpy as jnp
import numpy as np
from jax import lax
from jax.experimental import pallas as pl
from jax.experimental.pallas import tpu as pltpu

F32 = jnp.float32
BF16 = jnp.bfloat16

D_MODEL = 1024
D_GROUP = D_MODEL // 4
N_HEADS = 4
HEAD_DIM = D_GROUP // N_HEADS
D_FF = 4 * D_MODEL
N_MOD = 6
EPS = 1e-6
LB_FLOOR = 1e-30
LOG2E = 1.4426950408889634
GRID_W = 64
ROPE_BASE = 10000.0
S5_GROUP = 16
S5_GROUPS = D_GROUP // S5_GROUP
S5_STATE = 64
S5_LANES = S5_GROUPS * S5_STATE

BLK = 256
HG_CHUNK = 16
GDN_CHUNK = 64
GDN_SUB = 16
S5_CHUNK = 4
S5_PART = 256
FF_CHUNK = 1024
V7X_VMEM_LIMIT = 56 * 1024 * 1024

ZA_W = 5 * D_GROUP
ZR_W = 4 * D_GROUP
ZG_W = 4 * D_GROUP + 128
ZS_W = D_GROUP


def _cparams(sem):
    return pltpu.CompilerParams(dimension_semantics=sem, vmem_limit_bytes=V7X_VMEM_LIMIT)


def _bwd_block(s, nbc, nb):
    return jnp.where(s < nbc, nbc - 1 - s, nb - 1 - (s - nbc))


def _dot(a, b):
    return jnp.dot(a, b, preferred_element_type=F32)


def _dot_nt(a, b):
    return lax.dot_general(a, b, (((1,), (1,)), ((), ())), preferred_element_type=F32)


def _dot_tn(a, b):
    return lax.dot_general(a, b, (((0,), (0,)), ((), ())), preferred_element_type=F32)


def _bmm(a, b):
    return jnp.einsum('gij,gjk->gik', a, b, preferred_element_type=F32)


def _bmm_nt(a, b):
    return jnp.einsum('gik,gjk->gij', a, b, preferred_element_type=F32)


def _silu(x):
    return x * jax.nn.sigmoid(x)


def _head_eq(n):
    r = lax.shift_right_logical(lax.broadcasted_iota(jnp.int32, (n, n), 0), 6)
    c = lax.shift_right_logical(lax.broadcasted_iota(jnp.int32, (n, n), 1), 6)
    return r == c


def _seg_sum(x, e):
    hi = x.astype(BF16)
    lo = (x - hi.astype(F32)).astype(BF16)
    return _dot(hi, e) + _dot(lo, e)


def _rms_mod(x, g, shift, scale):
    h = x * lax.rsqrt(jnp.mean(x * x, axis=-1, keepdims=True) + EPS) * g
    return h * (1.0 + scale) + shift


def _chunk_cumsum(x, pos, c, rev):
    n = x.shape[0]
    sh = 1
    while sh < c:
        if rev:
            x = x + jnp.where(pos < c - sh, pltpu.roll(x, n - sh, 0), 0.0)
        else:
            x = x + jnp.where(pos >= sh, pltpu.roll(x, sh, 0), 0.0)
        sh *= 2
    return x


def _chunk_total(x, pos, c):
    n = x.shape[0]
    sh = 1
    while sh < c:
        x = x + jnp.where(pos >= sh, pltpu.roll(x, sh, 0), pltpu.roll(x, n - (c - sh), 0))
        sh *= 2
    return x


def _mod_kernel(c_ref, w_ref, b_ref, o_ref):
    o_ref[0] = _dot(_silu(c_ref[...]), w_ref[0]) + b_ref[0]


def _mod_proj(cvecs, mod_w, mod_b):
    depth, d, n = mod_w.shape
    rows = cvecs.shape[0]
    tn = 1536
    return pl.pallas_call(
        _mod_kernel,
        grid=(depth, n // tn),
        in_specs=[pl.BlockSpec((rows, d), lambda l, j: (0, 0)),
                  pl.BlockSpec((1, d, tn), lambda l, j: (l, 0, j)),
                  pl.BlockSpec((1, 1, tn), lambda l, j: (l, 0, j))],
        out_specs=pl.BlockSpec((1, rows, tn), lambda l, j: (l, 0, j)),
        out_shape=jax.ShapeDtypeStruct((depth, rows, n), F32),
        compiler_params=_cparams(("arbitrary", "arbitrary")),
        name="mod_proj",
    )(cvecs, mod_w, mod_b.reshape(depth, 1, n))


def _chunk_select(n_tok, tokens_on_rows):
    a = lax.broadcasted_iota(jnp.int32, (n_tok, n_tok), 1 if tokens_on_rows else 0)
    t = lax.broadcasted_iota(jnp.int32, (n_tok, n_tok), 0 if tokens_on_rows else 1)
    n_chunks = n_tok // S5_CHUNK
    shift = n_chunks.bit_length() - 1
    assert n_chunks == 1 << shift
    return (t == (a & (n_chunks - 1)) * S5_CHUNK + lax.shift_right_logical(a, shift)).astype(BF16)


def _in_proj_kernel(x_ref, mod_ref, g_ref, w_ref, wab_ref, ws_ref, za_ref, zr_ref, zg_ref, zs_ref, zc_ref):
    m = mod_ref[0, 0]
    h = _rms_mod(x_ref[0], g_ref[...], m[0:1], m[1:2]).astype(BF16)
    za_ref[0] = _dot(h, w_ref[:, 0:ZA_W])
    zr_ref[0] = _dot(h, w_ref[:, ZA_W:ZA_W + ZR_W])
    zg_ref[0, :, 0:ZG_W - 128] = _dot(h, w_ref[:, ZA_W + ZR_W:ZA_W + ZR_W + ZG_W - 128])
    zg_ref[0, :, ZG_W - 128:] = _dot(h, wab_ref[...])
    zs_ref[0] = _dot(h, ws_ref[...])
    zp = _dot(_chunk_select(BLK, False), zs_ref[0].astype(BF16)).astype(BF16)
    n_chunks = BLK // S5_CHUNK
    for pos in range(S5_CHUNK):
        zc_ref[0, :, pos * ZS_W:(pos + 1) * ZS_W] = zp[pos * n_chunks:(pos + 1) * n_chunks, :]


def _in_proj(xc, modv, g, w, w_ab, w_s, nbc):
    b, l, d = xc.shape
    nb = l // BLK
    row = lambda width: pl.BlockSpec((1, BLK, width), lambda i, j: (i, j, 0))
    const = lambda a: pl.BlockSpec(a.shape, lambda i, j: (0,) * a.ndim)
    return pl.pallas_call(
        _in_proj_kernel,
        grid=(b, nb),
        in_specs=[row(d),
                  pl.BlockSpec((1, 1, 8, d), lambda i, j: (i, jnp.where(j < nbc, 0, 1), 0, 0)),
                  pl.BlockSpec((1, d), lambda i, j: (0, 0)), const(w), const(w_ab), const(w_s)],
        out_specs=[row(ZA_W), row(ZR_W), row(ZG_W), row(ZS_W),
                   pl.BlockSpec((1, BLK // S5_CHUNK, S5_CHUNK * ZS_W), lambda i, j: (i, j, 0))],
        out_shape=[jax.ShapeDtypeStruct((b, l, width), F32) for width in (ZA_W, ZR_W, ZG_W, ZS_W)]
                  + [jax.ShapeDtypeStruct((b, l // S5_CHUNK, S5_CHUNK * ZS_W), BF16)],
        compiler_params=_cparams(("arbitrary", "arbitrary")),
        name="in_proj",
    )(xc, modv, g, w, w_ab, w_s)


def _hgrn_dir(q_raw, v, f_raw, lb, st, rev):
    n = q_raw.shape[0]
    c = HG_CHUNK
    pos = lax.broadcasted_iota(jnp.int32, (n, D_GROUP), 0) & (c - 1)
    heq = _head_eq(D_GROUP)
    e = heq.astype(BF16)
    qs = _silu(q_raw) * HEAD_DIM ** -0.5
    lbm = jnp.maximum(lb, LB_FLOOR)
    ex = jnp.exp(-jnp.abs(f_raw))
    inv = 1.0 / (1.0 + ex)
    sig_pos = jnp.where(f_raw >= 0, inv, ex * inv)
    sig_neg = jnp.where(f_raw >= 0, ex * inv, inv)
    logf = jnp.log(lbm * sig_neg + sig_pos) * LOG2E
    kk = (1.0 - lbm) * sig_neg
    bcum = _chunk_cumsum(logf, pos, c, rev)
    tot = _chunk_total(logf, pos, c)
    lk = jnp.log(kk) * LOG2E
    bk = bcum - lk

    nc, hc = n // c, c // 2
    split = lambda x: x.reshape(nc, 2, hc, D_GROUP)
    half = lambda x4, h: x4[:, h].reshape(nc * hc, D_GROUP)
    b4, bk4, lk4, v4, q4 = split(bcum), split(bk), split(lk), split(v), split(qs)
    bh = [half(b4, 0), half(b4, 1)]
    qh = [half(q4, 0), half(q4, 1)]
    pos8 = lax.broadcasted_iota(jnp.int32, (nc * hc, D_GROUP), 0) & (hc - 1)
    acc = [jnp.zeros((nc * hc, D_GROUP), F32), jnp.zeros((nc * hc, D_GROUP), F32)]
    qd = qs * jnp.exp2(bcum)
    kd = (kk * jnp.exp2(tot - bcum)).astype(BF16)
    vt = v.T.astype(BF16)
    cdec = jnp.exp2(tot)
    zeros = lambda r: jnp.zeros((r, D_GROUP), BF16)
    outs = [None] * nc
    for j in range(c):
        hj, jj = j // hc, j % hc
        row = lambda x4: jnp.broadcast_to(x4[:, hj, jj:jj + 1, :], (nc, hc, D_GROUP)).reshape(nc * hc, D_GROUP)
        bkj, lkj, vj = row(bk4), row(lk4), row(v4)
        other = 0 if rev else 1
        for ht in ((hj,) if hj == other else (hj, other)):
            term = qh[ht] * jnp.exp2(jnp.minimum(bh[ht] - bkj, lkj))
            contrib = _dot(term.astype(BF16), e) * vj
            if ht == hj:
                contrib = jnp.where((pos8 <= jj) if rev else (pos8 >= jj), contrib, 0.0)
            acc[ht] = acc[ht] + contrib

        ci = nc - 1 - j if rev else j
        rows = slice(ci * c, (ci + 1) * c)
        outs[ci] = _dot_nt(qd[rows], st)
        pieces = [zeros(ci * c), kd[rows], zeros(n - (ci + 1) * c)]
        kd_n = jnp.concatenate([p for p in pieces if p.shape[0]], axis=0)
        st = st * cdec[ci * c:ci * c + 1, :] + jnp.where(heq, _dot(vt, kd_n), 0.0)
    o = jnp.concatenate([acc[0].reshape(nc, 1, hc, D_GROUP), acc[1].reshape(nc, 1, hc, D_GROUP)],
                        axis=1).reshape(n, D_GROUP)
    return o + jnp.concatenate(outs, axis=0), st


def _hgrn_kernel(qf_ref, vf_ref, ff_ref, qb_ref, vb_ref, fb_ref, lb_ref, of_ref, ob_ref, st_ref):
    @pl.when(pl.program_id(1) == 0)
    def _():
        st_ref[...] = jnp.zeros_like(st_ref)

    o, st = _hgrn_dir(qf_ref[0], vf_ref[0], ff_ref[0], lb_ref[0:1, :], st_ref[0], False)
    of_ref[0] = o
    st_ref[0] = st
    o, st = _hgrn_dir(qb_ref[0], vb_ref[0], fb_ref[0], lb_ref[1:2, :], st_ref[1], True)
    ob_ref[0] = o
    st_ref[1] = st


def _scan_specs(nbc, nb, cols_f, cols_b, width=D_GROUP):
    fwd = [pl.BlockSpec((1, BLK, width), functools.partial(lambda i, s, c: (i, s, c), c=c)) for c in cols_f]
    bwd = [pl.BlockSpec((1, BLK, width), functools.partial(lambda i, s, c: (i, _bwd_block(s, nbc, nb), c), c=c))
           for c in cols_b]
    return fwd, bwd


def _hgrn_scan(za, lb, nbc):
    b, l, _ = za.shape
    nb = l // BLK
    fwd, bwd = _scan_specs(nbc, nb, (0, 1, 3), (0, 1, 4))
    outs = _scan_specs(nbc, nb, (0,), (0,))
    return pl.pallas_call(
        _hgrn_kernel,
        grid=(b, nb),
        in_specs=fwd + bwd + [pl.BlockSpec((2, D_GROUP), lambda i, s: (0, 0))],
        out_specs=outs[0] + outs[1],
        out_shape=[jax.ShapeDtypeStruct((b, l, D_GROUP), F32)] * 2,
        scratch_shapes=[pltpu.VMEM((2, D_GROUP, D_GROUP), F32)],
        compiler_params=_cparams(("arbitrary", "arbitrary")),
        name="hgrn_scan",
    )(za, za, za, za, za, za, lb)


def _rope(x, cos, sin_signed):
    lane = lax.broadcasted_iota(jnp.int32, x.shape, 1) & (HEAD_DIM - 1)
    half = HEAD_DIM // 2
    partner = jnp.where(lane < half, pltpu.roll(x, D_GROUP - half, 1), pltpu.roll(x, half, 1))
    return x * cos + partner * sin_signed


def _ret_dir(q, k, v, cos, sin, lg_ref, d, lg_lane, st, rev):
    n = q.shape[0]
    q = _rope(q, cos, sin)
    k = _rope(k, cos, sin) * HEAD_DIM ** -0.5
    ri = lax.broadcasted_iota(jnp.int32, (n, n), 0)
    ci = lax.broadcasted_iota(jnp.int32, (n, n), 1)
    rel = (ci - ri) if rev else (ri - ci)
    relf = jnp.maximum(rel, 0).astype(F32)
    t = lax.broadcasted_iota(jnp.int32, (n, D_GROUP), 0).astype(F32)
    lane_head = lax.shift_right_logical(lax.broadcasted_iota(jnp.int32, (n, D_GROUP), 1), 6)
    if rev:
        qdec, kdec = jnp.exp((n - t) * lg_lane), jnp.exp(t * lg_lane)
    else:
        qdec, kdec = jnp.exp((t + 1.0) * lg_lane), jnp.exp((n - 1.0 - t) * lg_lane)
    kb = k.astype(BF16)
    vb = v.astype(BF16)
    o = _dot((q * qdec).astype(BF16), st.astype(BF16))
    for h in range(N_HEADS):
        dmat = jnp.where(rel >= 0, jnp.exp(relf * lg_ref[d, h]), 0.0)
        s = _dot_nt(jnp.where(lane_head == h, q, 0.0).astype(BF16), kb)
        oh = _dot((s * dmat).astype(BF16), vb)
        o = o + jnp.where(lane_head == h, oh, 0.0)
    kv = _dot((k * kdec).T.astype(BF16), vb)
    st = st * jnp.exp(n * lg_lane) + jnp.where(_head_eq(D_GROUP), kv, 0.0)
    return o, st


def _ret_kernel(lg_ref, qf_ref, kf_ref, vf_ref, cf_ref, sf_ref, qb_ref, kb_ref, vb_ref, cb_ref, sb_ref,
                lgl_ref, of_ref, ob_ref, st_ref):
    @pl.when(pl.program_id(1) == 0)
    def _():
        st_ref[...] = jnp.zeros_like(st_ref)

    o, st = _ret_dir(qf_ref[0], kf_ref[0], vf_ref[0], cf_ref[...], sf_ref[...], lg_ref, 0,
                     lgl_ref[0:1, :], st_ref[0], False)
    of_ref[0] = o
    st_ref[0] = st
    o, st = _ret_dir(qb_ref[0], kb_ref[0], vb_ref[0], cb_ref[...], sb_ref[...], lg_ref, 1,
                     lgl_ref[1:2, :], st_ref[1], True)
    ob_ref[0] = o
    st_ref[1] = st


def _ret_scan(zr, cos_t, sin_t, lg, nbc):
    b, l, _ = zr.shape
    nb = l // BLK
    fwd, bwd = _scan_specs(nbc, nb, (0, 1, 2), (0, 1, 2))
    outs = _scan_specs(nbc, nb, (0,), (0,))
    tab_f = pl.BlockSpec((BLK, D_GROUP), lambda i, s: (s, 0))
    tab_b = pl.BlockSpec((BLK, D_GROUP), lambda i, s: (_bwd_block(s, nbc, nb), 0))
    lg_lane = jnp.repeat(lg, HEAD_DIM, axis=-1)
    return pl.pallas_call(
        _ret_kernel,
        grid=(b, nb),
        in_specs=[pl.BlockSpec(memory_space=pltpu.SMEM)] + fwd + [tab_f, tab_f] + bwd + [tab_b, tab_b]
                 + [pl.BlockSpec((2, D_GROUP), lambda i, s: (0, 0))],
        out_specs=outs[0] + outs[1],
        out_shape=[jax.ShapeDtypeStruct((b, l, D_GROUP), F32)] * 2,
        scratch_shapes=[pltpu.VMEM((2, D_GROUP, D_GROUP), F32)],
        compiler_params=_cparams(("arbitrary", "arbitrary")),
        name="ret_scan",
    )(lg, zr, zr, zr, cos_t, sin_t, zr, zr, zr, cos_t, sin_t, lg_lane)


def _gdn_prep_kernel(prev_ref, cur_ref, next_ref, ab_ref, cw_ref, an_ref, dtb_ref, qkv_ref, ga_ref, xs_ref,
                     *, nbc, nb):
    j = pl.program_id(1)
    is_ctx = j < nbc
    prev_ok = jnp.where(is_ctx, j > 0, j > nbc)
    next_ok = jnp.where(is_ctx, j < nbc - 1, j < nb - 1)
    n = BLK
    xs_ref[0:n, :] = jnp.where(prev_ok, prev_ref[0], 0.0)
    xs_ref[n:2 * n, :] = cur_ref[0]
    xs_ref[2 * n:3 * n, :] = jnp.where(next_ok, next_ref[0], 0.0)
    colpos = lax.broadcasted_iota(jnp.int32, (n, 1), 0) & (GRID_W - 1)
    acc = jnp.zeros((n, 3 * D_GROUP), F32)
    for dr in (-1, 0, 1):
        for dc in (-1, 0, 1):
            win = xs_ref[pl.ds(n + GRID_W * dr + dc, n), :]
            col_ok = (colpos >= 1) if dc == -1 else ((colpos <= GRID_W - 2) if dc == 1 else (colpos >= 0))
            ok = (is_ctx | col_ok) if dr == 0 else (jnp.logical_not(is_ctx) & col_ok)
            acc = acc + jnp.where(ok, win, 0.0) * cw_ref[(dr + 1) * 3 + (dc + 1):(dr + 1) * 3 + (dc + 2), :]
    xc = _silu(acc)
    e = _head_eq(D_GROUP).astype(BF16)
    q = xc[:, 0:D_GROUP]
    k = xc[:, D_GROUP:2 * D_GROUP]
    qkv_ref[0, :, 0:D_GROUP] = q * lax.rsqrt(_seg_sum(q * q, e) + EPS) * HEAD_DIM ** -0.5
    qkv_ref[0, :, D_GROUP:2 * D_GROUP] = k * lax.rsqrt(_seg_sum(k * k, e) + EPS)
    qkv_ref[0, :, 2 * D_GROUP:] = xc[:, 2 * D_GROUP:]
    ab = ab_ref[0]
    z = ab + dtb_ref[...]
    softplus = jnp.maximum(z, 0.0) + jnp.log(1.0 + jnp.exp(-jnp.abs(z)))
    lane = lax.broadcasted_iota(jnp.int32, ab.shape, 1)
    ga_ref[0] = jnp.where(lane < 2 * N_HEADS, an_ref[...] * softplus, jax.nn.sigmoid(ab))


def _gdn_prep(zg, conv_w, a_neg, dt_bias, nbc):
    b, l, _ = zg.shape
    nb = l // BLK
    w3 = 3 * D_GROUP
    return pl.pallas_call(
        functools.partial(_gdn_prep_kernel, nbc=nbc, nb=nb),
        grid=(b, nb),
        in_specs=[pl.BlockSpec((1, BLK, w3), lambda i, j: (i, jnp.maximum(j - 1, 0), 0)),
                  pl.BlockSpec((1, BLK, w3), lambda i, j: (i, j, 0)),
                  pl.BlockSpec((1, BLK, w3), lambda i, j: (i, jnp.minimum(j + 1, nb - 1), 0)),
                  pl.BlockSpec((1, BLK, 128), lambda i, j: (i, j, (ZG_W - 128) // 128)),
                  pl.BlockSpec((9, w3), lambda i, j: (0, 0)),
                  pl.BlockSpec((1, 128), lambda i, j: (0, 0)),
                  pl.BlockSpec((1, 128), lambda i, j: (0, 0))],
        out_specs=[pl.BlockSpec((1, BLK, w3), lambda i, j: (i, j, 0)),
                   pl.BlockSpec((1, BLK, 128), lambda i, j: (i, j, 0))],
        out_shape=[jax.ShapeDtypeStruct((b, l, w3), F32), jax.ShapeDtypeStruct((b, l, 128), F32)],
        scratch_shapes=[pltpu.VMEM((3 * BLK, w3), F32)],
        compiler_params=_cparams(("arbitrary", "arbitrary")),
        name="gdn_prep",
    )(zg, zg, zg, zg, conv_w, a_neg, dt_bias)


def _heads(x):
    return jnp.concatenate([x[:, h * HEAD_DIM:(h + 1) * HEAD_DIM][None] for h in range(N_HEADS)], axis=0)


def _col_heads(x, lane0):
    return jnp.concatenate([jnp.broadcast_to(x[:, lane0 + h:lane0 + h + 1], (GDN_CHUNK, HEAD_DIM))[None]
                            for h in range(N_HEADS)], axis=0)


def _row_heads(xt, row0):
    return jnp.concatenate([jnp.broadcast_to(xt[row0 + h:row0 + h + 1, :], (HEAD_DIM, GDN_CHUNK))[None]
                            for h in range(N_HEADS)], axis=0)


def _bmm16(a, b):
    return _bmm(a.astype(BF16), b.astype(BF16))


def _gdn_gates(ga, rev):
    pos = lax.broadcasted_iota(jnp.int32, ga.shape, 0) & (GDN_CHUNK - 1)
    g2 = _chunk_cumsum(ga, pos, GDN_CHUNK, rev)
    return g2, g2.T, _chunk_total(ga, pos, GDN_CHUNK)


def _gdn_load(group):
    qkv_ref, ga_ref, (g2, g2t, tot2), cidx, d = group
    c = GDN_CHUNK
    rows = slice(cidx * c, (cidx + 1) * c)
    a0 = d * N_HEADS
    b0 = 2 * N_HEADS + d * N_HEADS
    return dict(gc=_col_heads(g2[rows], a0), gr=_row_heads(g2t[:, rows], a0), totc=_col_heads(tot2[rows], a0),
                beta=_col_heads(ga_ref[0, rows, :], b0), q=_heads(qkv_ref[0, rows, 0:D_GROUP]),
                k=_heads(qkv_ref[0, rows, D_GROUP:2 * D_GROUP]), v=_heads(qkv_ref[0, rows, 2 * D_GROUP:]))


def _gdn_par(groups, n_fwd, out):
    c = GDN_CHUNK
    ops = [_gdn_load(g) for g in groups]
    cat = lambda name: jnp.concatenate([o[name] for o in ops], axis=0)
    gc, gr, totc, beta, q3, k3, v3 = (cat(nm) for nm in ('gc', 'gr', 'totc', 'beta', 'q', 'k', 'v'))
    n_inst = q3.shape[0]
    shape = (n_inst, c, c)
    rev = lax.broadcasted_iota(jnp.int32, shape, 0) >= n_fwd * N_HEADS
    ri = lax.broadcasted_iota(jnp.int32, shape, 1)
    ci = lax.broadcasted_iota(jnp.int32, shape, 2)
    lag = jnp.where(rev, ci - ri, ri - ci)
    tri = lag >= 0
    strict = lag > 0
    same_sub = lax.shift_right_logical(ri, 4) == lax.shift_right_logical(ci, 4)
    eye = (ri == ci).astype(F32)

    lmat = jnp.where(tri, jnp.exp(jnp.minimum(gc - gr, 0.0)), 0.0)
    kb = k3 * beta
    k16 = k3.astype(BF16)
    kkt = _bmm_nt(kb.astype(BF16), k16)
    qkt = _bmm_nt(q3.astype(BF16), k16)
    yield
    amat = jnp.where(strict, kkt * lmat, 0.0)
    dmat = jnp.where(same_sub, amat, 0.0)
    lo = amat - dmat
    d2 = _bmm16(dmat, dmat)
    yield
    d4 = _bmm16(d2, d2)
    p1 = _bmm16(eye - dmat, eye + d2)
    yield
    d8 = _bmm16(d4, d4)
    yield
    p2 = _bmm16(eye + d4, eye + d8)
    yield
    tdiag = _bmm16(p1, p2)
    yield
    mmat = _bmm16(tdiag, lo)
    yield
    m2 = _bmm16(mmat, mmat)
    yield
    p3 = _bmm16(eye - mmat, eye + m2)
    yield
    tinv = _bmm16(p3, tdiag).astype(BF16)
    yield
    out.update(u=_bmm(tinv, (v3 * beta).astype(BF16)),
               w=_bmm(tinv, (kb * jnp.exp(gc)).astype(BF16)).astype(BF16),
               qk=jnp.where(tri, qkt * lmat, 0.0).astype(BF16), qd=(q3 * jnp.exp(gc)).astype(BF16),
               kd=(k3 * jnp.exp(totc - gc)).astype(BF16), cdec=jnp.exp(totc))


def _gdn_seq(p, st_box, n_steps, write):
    nh = N_HEADS
    for k in range(n_steps):
        sel = lambda a: jnp.concatenate([a[nh * k:nh * (k + 1)], a[nh * (n_steps + k):nh * (n_steps + k + 1)]], axis=0)
        st = st_box[0]
        s16 = st.astype(BF16)
        ws = _bmm(sel(p['w']), s16)
        qs = _bmm(sel(p['qd']), s16)
        yield
        v16 = (sel(p['u']) - ws).astype(BF16)
        o = qs + _bmm(sel(p['qk']), v16)
        st_box[0] = st * sel(p['cdec']) + jnp.einsum('gck,gcv->gkv', sel(p['kd']), v16, preferred_element_type=F32)
        write(k, jnp.concatenate([o[h] for h in range(nh)], axis=-1),
              jnp.concatenate([o[nh + h] for h in range(nh)], axis=-1))
        yield


def _interleave(*gens):
    gens = list(gens)
    while gens:
        for g in list(gens):
            try:
                next(g)
            except StopIteration:
                gens.remove(g)


def _gdn_kernel(xf_ref, gf_ref, xb_ref, gb_ref, of_ref, ob_ref, st_ref):
    @pl.when(pl.program_id(1) == 0)
    def _():
        st_ref[...] = jnp.zeros_like(st_ref)

    c = GDN_CHUNK
    nc = xf_ref.shape[1] // c
    half = nc // 2
    gates_f = _gdn_gates(gf_ref[0], False)
    gates_b = _gdn_gates(gb_ref[0], True)
    fwd = lambda ci: (xf_ref, gf_ref, gates_f, ci, 0)
    bwd = lambda ci: (xb_ref, gb_ref, gates_b, ci, 1)

    def writer(first_f, first_b):
        def write(k, o_f, o_b):
            cf, cb = first_f + k, first_b - k
            of_ref[0, cf * c:(cf + 1) * c, :] = o_f
            ob_ref[0, cb * c:(cb + 1) * c, :] = o_b
        return write

    st_box = [jnp.concatenate([st_ref[0], st_ref[1]], axis=0)]
    p1, p2 = {}, {}
    _interleave(_gdn_par([fwd(t) for t in range(half)] + [bwd(nc - 1 - t) for t in range(half)], half, p1))
    _interleave(_gdn_par([fwd(half + t) for t in range(half)] + [bwd(nc - 1 - half - t) for t in range(half)],
                         half, p2),
                _gdn_seq(p1, st_box, half, writer(0, nc - 1)))
    _interleave(_gdn_seq(p2, st_box, half, writer(half, nc - 1 - half)))
    st_ref[0] = st_box[0][:N_HEADS]
    st_ref[1] = st_box[0][N_HEADS:]


def _gdn_scan(qkv, ga, nbc):
    b, l, w3 = qkv.shape
    nb = l // BLK
    xf, xb = _scan_specs(nbc, nb, (0,), (0,), w3)
    gf, gb = _scan_specs(nbc, nb, (0,), (0,), 128)
    outs = _scan_specs(nbc, nb, (0,), (0,))
    return pl.pallas_call(
        _gdn_kernel,
        grid=(b, nb),
        in_specs=xf + gf + xb + gb,
        out_specs=outs[0] + outs[1],
        out_shape=[jax.ShapeDtypeStruct((b, l, D_GROUP), F32)] * 2,
        scratch_shapes=[pltpu.VMEM((2, N_HEADS, HEAD_DIM, HEAD_DIM), F32)],
        compiler_params=_cparams(("arbitrary", "arbitrary")),
        name="gdn_scan",
    )(qkv, ga, qkv, ga)


def _cmul(ar, ai, hr, hi):
    return ar * hr - ai * hi, ar * hi + ai * hr


def _s5_pack(re, im):
    lead = re.shape[:-1]
    parts = jnp.stack([re.reshape(lead + (-1, S5_PART)), im.reshape(lead + (-1, S5_PART))], axis=-2)
    return parts.reshape(lead + (2 * S5_LANES,))


def _s5_kernel(u_ref, ws_ref, wc_ref, wt_ref, aux_ref, pw_ref, y_ref, h_ref, *, rev, parts):
    u = u_ref[0]
    rows, width = h_ref.shape
    pw = S5_PART
    pieces = [(slice(o, o + pw), slice(o + pw, o + 2 * pw)) for o in range(0, width, 2 * pw)]
    pos = lax.broadcasted_iota(jnp.int32, (rows // 8, 8, pw), 1)
    gw = pw // S5_STATE * S5_GROUP
    u_q = [jnp.concatenate([u[:, i * D_GROUP + q * gw:i * D_GROUP + (q + 1) * gw] for i in range(S5_CHUNK)], axis=1)
           for q in range(len(pieces))]
    for q, (re, im) in enumerate(pieces):
        x = _dot(u_q[q], ws_ref[q])
        xr = x[:, 0:pw].reshape(rows // 8, 8, pw)
        xi = x[:, pw:].reshape(rows // 8, 8, pw)
        for lvl, k in enumerate((1, 2, 4)):
            dr, di = _cmul(aux_ref[lvl:lvl + 1, re], aux_ref[lvl:lvl + 1, im],
                           pltpu.roll(xr, 8 - k if rev else k, 1), pltpu.roll(xi, 8 - k if rev else k, 1))
            valid = (pos < 8 - k) if rev else (pos >= k)
            xr = xr + jnp.where(valid, dr, 0.0)
            xi = xi + jnp.where(valid, di, 0.0)
        h_ref[:, re] = xr.reshape(rows, pw)
        h_ref[:, im] = xi.reshape(rows, pw)

    sub = lax.broadcasted_iota(jnp.int32, (8, pw), 0)
    edge = 0 if rev else 7

    def tile_step(t, carry):
        sl = pl.ds(pl.multiple_of(t * 8, 8), 8)
        out = []
        for (re, im), (cr, ci) in zip(pieces, carry):
            dr, di = _cmul(pw_ref[:, re], pw_ref[:, im], cr, ci)
            fr, fi = h_ref[sl, re] + dr, h_ref[sl, im] + di
            first = sub == (7 if rev else 0)
            h_ref[sl, re] = jnp.where(first, cr, pltpu.roll(fr, 7 if rev else 1, 0))
            h_ref[sl, im] = jnp.where(first, ci, pltpu.roll(fi, 7 if rev else 1, 0))
            out.append((jnp.broadcast_to(fr[edge:edge + 1, :], fr.shape),
                        jnp.broadcast_to(fi[edge:edge + 1, :], fi.shape)))
        return tuple(out)

    carry = tuple((jnp.zeros((8, pw), F32), jnp.zeros((8, pw), F32)) for _ in pieces)
    for row0, nrows in parts:
        t0, nt = row0 // 8, nrows // 8
        if rev:
            carry = lax.fori_loop(0, nt, lambda i, c, t0=t0, nt=nt: tile_step(t0 + nt - 1 - i, c), carry)
        else:
            carry = lax.fori_loop(0, nt, lambda i, c, t0=t0: tile_step(t0 + i, c), carry)
    y_q = [_dot(u_q[q], wt_ref[q]) + _dot(h_ref[:, re.start:im.stop].astype(BF16), wc_ref[q])
           for q, (re, im) in enumerate(pieces)]
    y_pos = [jnp.concatenate([y_q[q][:, i * gw:(i + 1) * gw] for q in range(len(pieces))], axis=1).astype(BF16)
             for i in range(S5_CHUNK)]
    perm = _chunk_select(BLK, True)
    n_chunks = BLK // S5_CHUNK
    for blk in range(rows // n_chunks):
        stacked = jnp.concatenate([y[blk * n_chunks:(blk + 1) * n_chunks] for y in y_pos], axis=0)
        y_ref[0, blk * BLK:(blk + 1) * BLK, :] = _dot(perm, stacked).astype(BF16)


def _s5_scan(u_c, weights, li, dd, parts):
    bsz, rows, w = u_c.shape
    pick = lambda a: pl.BlockSpec((None, None) + a.shape[2:], lambda b: (li, dd) + (0,) * (a.ndim - 2))
    n_tok = rows * S5_CHUNK
    return pl.pallas_call(
        functools.partial(_s5_kernel, rev=dd == 1, parts=parts),
        grid=(bsz,),
        in_specs=[pl.BlockSpec((1, rows, w), lambda b: (b, 0, 0))] + [pick(a) for a in weights],
        out_specs=pl.BlockSpec((1, n_tok, D_GROUP), lambda b: (b, 0, 0)),
        out_shape=jax.ShapeDtypeStruct((bsz, n_tok, D_GROUP), BF16),
        scratch_shapes=[pltpu.VMEM((rows, 2 * S5_LANES), F32)],
        compiler_params=_cparams(("arbitrary",)),
        name="s5_scan_bwd" if dd == 1 else "s5_scan_fwd",
    )(u_c, *weights)


def _s5_weights(lam_re, lam_im, log_dt, b_re, b_im, c_re, c_im):
    cch, ng = S5_CHUNK, S5_GROUPS
    nq = S5_LANES // S5_PART
    gl = ng // nq
    dt = jnp.exp(log_dt)[..., None]
    ang, dec = lam_im * dt, lam_re * dt

    def power(n_fwd, n_bwd=None):
        n = np.stack([n_fwd, n_fwd if n_bwd is None else n_bwd], axis=1).astype(np.float32)
        n = jnp.asarray(n).reshape(-1, 1, 2, 1, 1)
        mag = jnp.exp(n * dec)
        return mag * jnp.cos(n * ang), mag * jnp.sin(n * ang)

    pr, pi = power(np.arange(cch + 1))
    ar, ai = pr[1], pi[1]
    den = lam_re * lam_re + lam_im * lam_im
    nr, ni = ar - 1.0, ai
    fr = (nr * lam_re + ni * lam_im) / den
    fi = (ni * lam_re - nr * lam_im) / den
    bbr = fr[..., None] * b_re[:, None] - fi[..., None] * b_im[:, None]
    bbi = fr[..., None] * b_im[:, None] + fi[..., None] * b_re[:, None]
    eye = jnp.eye(gl, dtype=F32)
    pieces = lambda x, axis: x.reshape(x.shape[:axis] + (nq, gl) + x.shape[axis + 1:])
    idx = np.arange(cch)

    af_r, af_i = power(cch - 1 - idx, idx)
    ws = jnp.stack([af_r[..., None] * bbr - af_i[..., None] * bbi, af_r[..., None] * bbi + af_i[..., None] * bbr], -1)
    ws = jnp.einsum('ndrqgpcs,gh->drqngcshp', pieces(ws, 3), eye)
    ws = ws.reshape(ws.shape[:3] + (cch * gl * S5_GROUP, 2 * S5_PART))
    up_r, up_i = power(idx + 1, cch - idx)
    cr, ci = c_re[:, None], c_im[:, None]
    wc = jnp.stack([cr * up_r[..., None, :] - ci * up_i[..., None, :],
                    -(cr * up_i[..., None, :] + ci * up_r[..., None, :])], -1)
    wc = jnp.einsum('ndrqgcps,gh->drqshpngc', pieces(wc, 3), eye)
    wc = wc.reshape(wc.shape[:3] + (2 * S5_PART, cch * gl * S5_GROUP))
    ca_r = cr * pr[:cch, ..., None, :] - ci * pi[:cch, ..., None, :]
    ca_i = cr * pi[:cch, ..., None, :] + ci * pr[:cch, ..., None, :]
    tap = functools.partial(jnp.einsum, 'ndrgcp,drgpe->ndrgec', precision=lax.Precision.HIGHEST)
    taps = tap(ca_r, bbr) - tap(ca_i, bbi)
    lag = idx[None, :] - idx[:, None]
    tl = jnp.where((lag >= 0).reshape(cch, cch, 1, 1, 1, 1, 1), taps[np.maximum(lag, 0)], 0.0)
    is_bwd = (np.arange(2) == 1).reshape(1, 1, 1, 2, 1, 1, 1)
    wt = jnp.where(is_bwd, jnp.swapaxes(tl, 0, 1), tl)
    wt = jnp.einsum('jidrqgec,gh->drqjgeihc', pieces(wt, 4), eye)
    wt = wt.reshape(wt.shape[:3] + (cch * gl * S5_GROUP, cch * gl * S5_GROUP))

    def table(n_fwd, n_bwd):
        flat = lambda t: t.reshape(t.shape[:3] + (S5_LANES,))
        tr, ti = power(n_fwd, n_bwd)
        return jnp.moveaxis(_s5_pack(flat(tr), flat(ti)), 0, 2)

    doubling = cch * np.array([1, 2, 4, 0, 0, 0, 0, 0])
    aux = table(doubling, doubling)
    pw = table(cch * (np.arange(8) + 1), cch * (8 - np.arange(8)))
    return ws.astype(BF16), wc.astype(BF16), wt.astype(BF16), aux, pw


def _gated_norm(o, gate, gain, e):
    y = o * lax.rsqrt(_seg_sum(o * o, e) * (1.0 / HEAD_DIM) + EPS)
    if gain is not None:
        y = y * gain
    return y * _silu(gate)


def _gelu_tanh(x):
    return 0.5 * x * (1.0 + jnp.tanh(np.sqrt(2.0 / np.pi) * (x + 0.044715 * (x * x * x))))


def _out_mlp_kernel(x_ref, mod_ref, haf_ref, hab_ref, hg_ref, rf_ref, rb_ref, rg_ref, gf_ref, gb_ref, gg_ref,
                    sf_ref, sb_ref, su_ref, hng_ref, gng_ref, sd_ref, glw_ref, glb_ref, wo_ref, n2_ref, w1_ref,
                    w2_ref, fg_ref, o_ref, *, final):
    m = mod_ref[0, 0]
    e = _head_eq(D_GROUP).astype(BF16)
    a = _gated_norm(haf_ref[0] + hab_ref[0], hg_ref[0], hng_ref[...], e)
    r = _gated_norm(rf_ref[0] + rb_ref[0], rg_ref[0], None, e)
    g = _gated_norm(gf_ref[0] + gb_ref[0], gg_ref[0], gng_ref[...], e)
    s = _gelu_tanh(sf_ref[0].astype(F32) + sb_ref[0].astype(F32) + su_ref[0] * sd_ref[...])
    s = s * jax.nn.sigmoid(_dot(s, glw_ref[...]) + glb_ref[...])
    y = jnp.zeros((x_ref.shape[1], D_MODEL), F32)
    for i, part in enumerate((a, r, g, s)):
        y = y + _dot(part.astype(BF16), wo_ref[i * D_GROUP:(i + 1) * D_GROUP, :])
    x1 = x_ref[0] + m[2:3] * y
    h2 = _rms_mod(x1, n2_ref[...], m[3:4], m[4:5]).astype(BF16)
    acc = jnp.zeros_like(x1)
    for c0 in range(0, D_FF, FF_CHUNK):
        hid = jnp.maximum(_dot(h2, w1_ref[:, c0:c0 + FF_CHUNK]), 0.0)
        acc = acc + _dot((hid * hid).astype(BF16), w2_ref[c0:c0 + FF_CHUNK, :])
    x2 = x1 + m[5:6] * acc
    if final:
        x2 = x2 * lax.rsqrt(jnp.mean(x2 * x2, axis=-1, keepdims=True) + EPS) * fg_ref[...]
    o_ref[0] = x2


def _out_mlp(xc, modv, za, zr, zg, zs, ha, ra, ga, sa, hng, gng, sd, glw, glb, wo, n2, w1, w2, fg, nbc, final):
    b, l, d = xc.shape
    nb = l // BLK
    off = nbc if final else 0
    rows = lambda width, c=0: pl.BlockSpec((1, BLK, width), lambda i, j: (i, j + off, c))
    const = lambda arr: pl.BlockSpec(arr.shape, lambda i, j: (0,) * arr.ndim)
    seg = lambda i, j: (i, jnp.where(j + off < nbc, 0, 1), 0, 0)
    weights = [hng, gng, sd, glw, glb, wo, n2, w1, w2, fg]
    args = [xc, modv, ha[0], ha[1], za, ra[0], ra[1], zr, ga[0], ga[1], zg, sa[0], sa[1], zs] + weights
    in_specs = [rows(d), pl.BlockSpec((1, 1, 8, d), seg),
                rows(D_GROUP), rows(D_GROUP), rows(D_GROUP, 2),
                rows(D_GROUP), rows(D_GROUP), rows(D_GROUP, 3),
                rows(D_GROUP), rows(D_GROUP), rows(D_GROUP, 3),
                rows(D_GROUP), rows(D_GROUP), rows(D_GROUP)] + [const(a) for a in weights]
    return pl.pallas_call(
        functools.partial(_out_mlp_kernel, final=final),
        grid=(b, nb - off),
        in_specs=in_specs,
        out_specs=pl.BlockSpec((1, BLK, d), lambda i, j: (i, j, 0)),
        out_shape=jax.ShapeDtypeStruct((b, l - off * BLK, d), F32),
        compiler_params=_cparams(("arbitrary", "arbitrary")),
        name="out_mlp_final" if final else "out_mlp",
    )(*args)


def kernel(x, c, ctx, c_ctx, mod_w, mod_b, norm1_g, norm2_g, w_in, hgrn_lb_logits, hgrn_norm_g, ret_decay_logit,
           gdn_conv_w, gdn_a_log, gdn_dt_bias, gdn_norm_g, s5_lam_re, s5_lam_im, s5_log_dt, s5_b_re, s5_b_im,
           s5_c_re, s5_c_im, s5_d, s5_glu_w, s5_glu_b, w_out, mlp_w1, mlp_w2, final_norm_g):
    bsz, t_lat, d = x.shape
    t_ctx = ctx.shape[1]
    depth = mod_w.shape[0]
    assert d == D_MODEL and t_ctx % BLK == 0 and t_lat % BLK == 0 and t_lat % GRID_W == 0
    assert t_ctx % (8 * S5_CHUNK) == 0 and t_lat % (8 * S5_CHUNK) == 0
    l = t_ctx + t_lat
    nbc = t_ctx // BLK

    xc = jnp.concatenate([ctx, x], axis=1).astype(F32)

    n_rows = -(-(bsz + 1) // 8) * 8
    cvecs = jnp.zeros((n_rows, d), F32).at[:bsz].set(c.astype(F32)).at[bsz].set(c_ctx.astype(F32))
    mod = _mod_proj(cvecs, mod_w.astype(F32), mod_b.astype(F32)).reshape(depth, n_rows, N_MOD, d)
    mod_lat = mod[:, :bsz]
    mod_ctx = jnp.broadcast_to(mod[:, bsz:bsz + 1], mod_lat.shape)
    modv = jnp.stack([mod_ctx, mod_lat], axis=2)
    modv = jnp.pad(modv, ((0, 0), (0, 0), (0, 0), (0, 8 - N_MOD), (0, 0)))

    sm = jax.nn.softmax(hgrn_lb_logits.astype(F32), axis=0)
    lbs = jnp.cumsum(sm, axis=0) - sm[:1]
    pos = jnp.arange(l, dtype=F32)
    half = HEAD_DIM // 2
    inv = ROPE_BASE ** (-jnp.arange(half, dtype=F32) / half)
    ang = pos[:, None] * inv[None, :]
    cos_t = jnp.tile(jnp.cos(ang), (1, 2 * N_HEADS))
    sin_t = jnp.tile(jnp.concatenate([-jnp.sin(ang), jnp.sin(ang)], axis=1), (1, N_HEADS))
    n_main = ZA_W + ZR_W + ZG_W - 128
    w_main = w_in.astype(BF16)
    w_ab = jnp.pad(w_main[..., n_main:n_main + 4 * N_HEADS], ((0, 0), (0, 0), (0, 128 - 4 * N_HEADS)))
    w_s = w_main[..., n_main + 4 * N_HEADS:]
    pad8 = lambda v: jnp.pad(v.astype(F32).reshape(1, 2 * N_HEADS), ((0, 0), (0, 128 - 2 * N_HEADS)))
    s5_w = _s5_weights(*(p.astype(F32) for p in (s5_lam_re, s5_lam_im, s5_log_dt, s5_b_re, s5_b_im, s5_c_re, s5_c_im)))
    s5_parts = ((0, t_ctx // S5_CHUNK), (t_ctx // S5_CHUNK, t_lat // S5_CHUNK))

    out = None
    for li in range(depth):
        final = li == depth - 1
        za, zr, zg, zs, u_c = _in_proj(xc, modv[li], norm1_g[li].astype(F32).reshape(1, d),
                                       w_main[li], w_ab[li], w_s[li], nbc)
        ha = _hgrn_scan(za, lbs[li], nbc)
        lg = jax.nn.log_sigmoid(ret_decay_logit[li].astype(F32))
        ra = _ret_scan(zr, cos_t, sin_t, lg, nbc)
        qkv, gab = _gdn_prep(zg, gdn_conv_w[li].astype(F32).reshape(9, 3 * D_GROUP),
                             pad8(-jnp.exp(gdn_a_log[li].astype(F32))), pad8(gdn_dt_bias[li]), nbc)
        ga = _gdn_scan(qkv, gab, nbc)
        sa = [_s5_scan(u_c, s5_w, li, dd, s5_parts) for dd in range(2)]
        row = lambda v: v.astype(F32).reshape(1, -1)
        res = _out_mlp(xc, modv[li], za, zr, zg, zs, ha, ra, ga, sa,
                       row(jnp.tile(hgrn_norm_g[li], N_HEADS)), row(jnp.tile(gdn_norm_g[li], N_HEADS)),
                       row(s5_d[li]), s5_glu_w[li].astype(F32), row(s5_glu_b[li]),
                       w_out[li].astype(BF16), row(norm2_g[li]), mlp_w1[li].astype(BF16), mlp_w2[li].astype(BF16),
                       row(final_norm_g), nbc, final)
        if final:
            out = res
        else:
            xc = res
    return out.astype(x.dtype)
```

```python
import functools

import jax
import jax.numpy as jnp
import numpy as np
from jax import lax
from jax.experimental import pallas as pl
from jax.experimental.pallas import tpu as pltpu

F32 = jnp.float32
BF16 = jnp.bfloat16

D_MODEL = 1024
D_GROUP = D_MODEL // 4
N_HEADS = 4
HEAD_DIM = D_GROUP // N_HEADS
D_FF = 4 * D_MODEL
N_MOD = 6
EPS = 1e-6
LB_FLOOR = 1e-30
LOG2E = 1.4426950408889634
GRID_W = 64
ROPE_BASE = 10000.0
S5_GROUP = 16
S5_GROUPS = D_GROUP // S5_GROUP
S5_STATE = 64
S5_LANES = S5_GROUPS * S5_STATE

BLK = 256
HG_CHUNK = 16
GDN_CHUNK = 64
GDN_SUB = 16
S5_CHUNK = 4
S5_PART = 256
FF_CHUNK = 1024
V7X_VMEM_LIMIT = 56 * 1024 * 1024

ZA_W = 5 * D_GROUP
ZR_W = 4 * D_GROUP
ZG_W = 4 * D_GROUP + 128
ZS_W = D_GROUP


def _cparams(sem):
    return pltpu.CompilerParams(dimension_semantics=sem, vmem_limit_bytes=V7X_VMEM_LIMIT)


def _bwd_block(s, nbc, nb):
    return jnp.where(s < nbc, nbc - 1 - s, nb - 1 - (s - nbc))


def _dot(a, b):
    return jnp.dot(a, b, preferred_element_type=F32)


def _dot_nt(a, b):
    return lax.dot_general(a, b, (((1,), (1,)), ((), ())), preferred_element_type=F32)


def _dot_tn(a, b):
    return lax.dot_general(a, b, (((0,), (0,)), ((), ())), preferred_element_type=F32)


def _bmm(a, b):
    return jnp.einsum('gij,gjk->gik', a, b, preferred_element_type=F32)


def _bmm_nt(a, b):
    return jnp.einsum('gik,gjk->gij', a, b, preferred_element_type=F32)


def _silu(x):
    return x * jax.nn.sigmoid(x)


def _head_eq(n):
    r = lax.shift_right_logical(lax.broadcasted_iota(jnp.int32, (n, n), 0), 6)
    c = lax.shift_right_logical(lax.broadcasted_iota(jnp.int32, (n, n), 1), 6)
    return r == c


def _seg_sum(x, e):
    hi = x.astype(BF16)
    lo = (x - hi.astype(F32)).astype(BF16)
    return _dot(hi, e) + _dot(lo, e)


def _rms_mod(x, g, shift, scale):
    h = x * lax.rsqrt(jnp.mean(x * x, axis=-1, keepdims=True) + EPS) * g
    return h * (1.0 + scale) + shift


def _chunk_cumsum(x, pos, c, rev):
    n = x.shape[0]
    sh = 1
    while sh < c:
        if rev:
            x = x + jnp.where(pos < c - sh, pltpu.roll(x, n - sh, 0), 0.0)
        else:
            x = x + jnp.where(pos >= sh, pltpu.roll(x, sh, 0), 0.0)
        sh *= 2
    return x


def _chunk_total(x, pos, c):
    n = x.shape[0]
    sh = 1
    while sh < c:
        x = x + jnp.where(pos >= sh, pltpu.roll(x, sh, 0), pltpu.roll(x, n - (c - sh), 0))
        sh *= 2
    return x


def _mod_kernel(c_ref, w_ref, b_ref, o_ref):
    o_ref[0] = _dot(_silu(c_ref[...]), w_ref[0]) + b_ref[0]


def _mod_proj(cvecs, mod_w, mod_b):
    depth, d, n = mod_w.shape
    rows = cvecs.shape[0]
    tn = 1536
    return pl.pallas_call(
        _mod_kernel,
        grid=(depth, n // tn),
        in_specs=[pl.BlockSpec((rows, d), lambda l, j: (0, 0)),
                  pl.BlockSpec((1, d, tn), lambda l, j: (l, 0, j)),
                  pl.BlockSpec((1, 1, tn), lambda l, j: (l, 0, j))],
        out_specs=pl.BlockSpec((1, rows, tn), lambda l, j: (l, 0, j)),
        out_shape=jax.ShapeDtypeStruct((depth, rows, n), F32),
        compiler_params=_cparams(("arbitrary", "arbitrary")),
        name="mod_proj",
    )(cvecs, mod_w, mod_b.reshape(depth, 1, n))


def _chunk_select(n_tok, tokens_on_rows):
    a = lax.broadcasted_iota(jnp.int32, (n_tok, n_tok), 1 if tokens_on_rows else 0)
    t = lax.broadcasted_iota(jnp.int32, (n_tok, n_tok), 0 if tokens_on_rows else 1)
    n_chunks = n_tok // S5_CHUNK
    shift = n_chunks.bit_length() - 1
    assert n_chunks == 1 << shift
    return (t == (a & (n_chunks - 1)) * S5_CHUNK + lax.shift_right_logical(a, shift)).astype(BF16)


def _seq_specs(src, nbc, off):
    ctx_arr, lat_arr, shift = src
    d = ctx_arr.shape[-1]
    return [pl.BlockSpec((1, BLK, d), lambda i, j: (i, jnp.minimum(j + off, nbc - 1), 0)),
            pl.BlockSpec((1, BLK, d), lambda i, j: (i, jnp.maximum(j + off - shift, 0), 0))]


def _in_proj_kernel(c_ref, x_ref, mod_ref, g_ref, w_ref, wab_ref, ws_ref, za_ref, zr_ref, zg_ref, zs_ref, zc_ref, *,
                    nbc):
    m = mod_ref[0, 0]
    x = jnp.where(pl.program_id(1) < nbc, c_ref[0], x_ref[0])
    h = _rms_mod(x, g_ref[...], m[0:1], m[1:2]).astype(BF16)
    za_ref[0] = _dot(h, w_ref[:, 0:ZA_W])
    zr_ref[0] = _dot(h, w_ref[:, ZA_W:ZA_W + ZR_W])
    zg_ref[0, :, 0:ZG_W - 128] = _dot(h, w_ref[:, ZA_W + ZR_W:ZA_W + ZR_W + ZG_W - 128])
    zg_ref[0, :, ZG_W - 128:] = _dot(h, wab_ref[...])
    zs_ref[0] = _dot(h, ws_ref[...])
    zp = _dot(_chunk_select(BLK, False), zs_ref[0].astype(BF16)).astype(BF16)
    n_chunks = BLK // S5_CHUNK
    for pos in range(S5_CHUNK):
        zc_ref[0, :, pos * ZS_W:(pos + 1) * ZS_W] = zp[pos * n_chunks:(pos + 1) * n_chunks, :]


def _in_proj(src, l, modv, g, w, w_ab, w_s, nbc):
    b, _, d = src[0].shape
    nb = l // BLK
    row = lambda width: pl.BlockSpec((1, BLK, width), lambda i, j: (i, j, 0))
    const = lambda a: pl.BlockSpec(a.shape, lambda i, j: (0,) * a.ndim)
    return pl.pallas_call(
        functools.partial(_in_proj_kernel, nbc=nbc),
        grid=(b, nb),
        in_specs=_seq_specs(src, nbc, 0) + [
                  pl.BlockSpec((1, 1, 8, d), lambda i, j: (i, jnp.where(j < nbc, 0, 1), 0, 0)),
                  pl.BlockSpec((1, d), lambda i, j: (0, 0)), const(w), const(w_ab), const(w_s)],
        out_specs=[row(ZA_W), row(ZR_W), row(ZG_W), row(ZS_W),
                   pl.BlockSpec((1, BLK // S5_CHUNK, S5_CHUNK * ZS_W), lambda i, j: (i, j, 0))],
        out_shape=[jax.ShapeDtypeStruct((b, l, width), F32) for width in (ZA_W, ZR_W, ZG_W, ZS_W)]
                  + [jax.ShapeDtypeStruct((b, l // S5_CHUNK, S5_CHUNK * ZS_W), BF16)],
        compiler_params=_cparams(("arbitrary", "arbitrary")),
        name="in_proj",
    )(src[0], src[1], modv, g, w, w_ab, w_s)


def _hgrn_dir(q_raw, v, f_raw, lb, st, rev):
    n = q_raw.shape[0]
    c = HG_CHUNK
    pos = lax.broadcasted_iota(jnp.int32, (n, D_GROUP), 0) & (c - 1)
    heq = _head_eq(D_GROUP)
    e = heq.astype(BF16)
    qs = _silu(q_raw) * HEAD_DIM ** -0.5
    lbm = jnp.maximum(lb, LB_FLOOR)
    ex = jnp.exp(-jnp.abs(f_raw))
    inv = 1.0 / (1.0 + ex)
    sig_pos = jnp.where(f_raw >= 0, inv, ex * inv)
    sig_neg = jnp.where(f_raw >= 0, ex * inv, inv)
    logf = jnp.log(lbm * sig_neg + sig_pos) * LOG2E
    kk = (1.0 - lbm) * sig_neg
    bcum = _chunk_cumsum(logf, pos, c, rev)
    tot = _chunk_total(logf, pos, c)
    lk = jnp.log(kk) * LOG2E
    bk = bcum - lk

    nc, hc = n // c, c // 2
    split = lambda x: x.reshape(nc, 2, hc, D_GROUP)
    half = lambda x4, h: x4[:, h].reshape(nc * hc, D_GROUP)
    b4, bk4, lk4, v4, q4 = split(bcum), split(bk), split(lk), split(v), split(qs)
    bh = [half(b4, 0), half(b4, 1)]
    qh = [half(q4, 0), half(q4, 1)]
    pos8 = lax.broadcasted_iota(jnp.int32, (nc * hc, D_GROUP), 0) & (hc - 1)
    acc = [jnp.zeros((nc * hc, D_GROUP), F32), jnp.zeros((nc * hc, D_GROUP), F32)]
    qd = qs * jnp.exp2(bcum)
    kd = (kk * jnp.exp2(tot - bcum)).astype(BF16)
    vt = v.T.astype(BF16)
    cdec = jnp.exp2(tot)
    zeros = lambda r: jnp.zeros((r, D_GROUP), BF16)
    outs = [None] * nc
    for j in range(c):
        hj, jj = j // hc, j % hc
        row = lambda x4: jnp.broadcast_to(x4[:, hj, jj:jj + 1, :], (nc, hc, D_GROUP)).reshape(nc * hc, D_GROUP)
        bkj, lkj, vj = row(bk4), row(lk4), row(v4)
        other = 0 if rev else 1
        for ht in ((hj,) if hj == other else (hj, other)):
            term = qh[ht] * jnp.exp2(jnp.minimum(bh[ht] - bkj, lkj))
            contrib = _dot(term.astype(BF16), e) * vj
            if ht == hj:
                contrib = jnp.where((pos8 <= jj) if rev else (pos8 >= jj), contrib, 0.0)
            acc[ht] = acc[ht] + contrib

        ci = nc - 1 - j if rev else j
        rows = slice(ci * c, (ci + 1) * c)
        outs[ci] = _dot_nt(qd[rows], st)
        pieces = [zeros(ci * c), kd[rows], zeros(n - (ci + 1) * c)]
        kd_n = jnp.concatenate([p for p in pieces if p.shape[0]], axis=0)
        st = st * cdec[ci * c:ci * c + 1, :] + jnp.where(heq, _dot(vt, kd_n), 0.0)
    o = jnp.concatenate([acc[0].reshape(nc, 1, hc, D_GROUP), acc[1].reshape(nc, 1, hc, D_GROUP)],
                        axis=1).reshape(n, D_GROUP)
    return o + jnp.concatenate(outs, axis=0), st


def _hgrn_kernel(qf_ref, vf_ref, ff_ref, qb_ref, vb_ref, fb_ref, lb_ref, of_ref, ob_ref, st_ref):
    @pl.when(pl.program_id(1) == 0)
    def _():
        st_ref[...] = jnp.zeros_like(st_ref)

    o, st = _hgrn_dir(qf_ref[0], vf_ref[0], ff_ref[0], lb_ref[0:1, :], st_ref[0], False)
    of_ref[0] = o
    st_ref[0] = st
    o, st = _hgrn_dir(qb_ref[0], vb_ref[0], fb_ref[0], lb_ref[1:2, :], st_ref[1], True)
    ob_ref[0] = o
    st_ref[1] = st


def _scan_specs(nbc, nb, cols_f, cols_b, width=D_GROUP):
    fwd = [pl.BlockSpec((1, BLK, width), functools.partial(lambda i, s, c: (i, s, c), c=c)) for c in cols_f]
    bwd = [pl.BlockSpec((1, BLK, width), functools.partial(lambda i, s, c: (i, _bwd_block(s, nbc, nb), c), c=c))
           for c in cols_b]
    return fwd, bwd


def _hgrn_scan(za, lb, nbc):
    b, l, _ = za.shape
    nb = l // BLK
    fwd, bwd = _scan_specs(nbc, nb, (0, 1, 3), (0, 1, 4))
    outs = _scan_specs(nbc, nb, (0,), (0,))
    return pl.pallas_call(
        _hgrn_kernel,
        grid=(b, nb),
        in_specs=fwd + bwd + [pl.BlockSpec((2, D_GROUP), lambda i, s: (0, 0))],
        out_specs=outs[0] + outs[1],
        out_shape=[jax.ShapeDtypeStruct((b, l, D_GROUP), F32)] * 2,
        scratch_shapes=[pltpu.VMEM((2, D_GROUP, D_GROUP), F32)],
        compiler_params=_cparams(("arbitrary", "arbitrary")),
        name="hgrn_scan",
    )(za, za, za, za, za, za, lb)


def _rope(x, cos, sin_signed):
    lane = lax.broadcasted_iota(jnp.int32, x.shape, 1) & (HEAD_DIM - 1)
    half = HEAD_DIM // 2
    partner = jnp.where(lane < half, pltpu.roll(x, D_GROUP - half, 1), pltpu.roll(x, half, 1))
    return x * cos + partner * sin_signed


def _ret_dir(q, k, v, cos, sin, lg_ref, d, lg_lane, st, rev):
    n = q.shape[0]
    q = _rope(q, cos, sin)
    k = _rope(k, cos, sin) * HEAD_DIM ** -0.5
    ri = lax.broadcasted_iota(jnp.int32, (n, n), 0)
    ci = lax.broadcasted_iota(jnp.int32, (n, n), 1)
    rel = (ci - ri) if rev else (ri - ci)
    relf = jnp.maximum(rel, 0).astype(F32)
    t = lax.broadcasted_iota(jnp.int32, (n, D_GROUP), 0).astype(F32)
    lane_head = lax.shift_right_logical(lax.broadcasted_iota(jnp.int32, (n, D_GROUP), 1), 6)
    if rev:
        qdec, kdec = jnp.exp((n - t) * lg_lane), jnp.exp(t * lg_lane)
    else:
        qdec, kdec = jnp.exp((t + 1.0) * lg_lane), jnp.exp((n - 1.0 - t) * lg_lane)
    kb = k.astype(BF16)
    vb = v.astype(BF16)
    o = _dot((q * qdec).astype(BF16), st.astype(BF16))
    q_heads = jnp.concatenate([jnp.where(lane_head == h, q, 0.0).astype(BF16) for h in range(N_HEADS)], axis=0)
    dmats = jnp.concatenate([jnp.where(rel >= 0, jnp.exp(relf * lg_ref[d, h]), 0.0) for h in range(N_HEADS)], axis=0)
    o_heads = _dot((_dot_nt(q_heads, kb) * dmats).astype(BF16), vb)
    for h in range(N_HEADS):
        o = o + jnp.where(lane_head == h, o_heads[h * n:(h + 1) * n], 0.0)
    kv = _dot((k * kdec).T.astype(BF16), vb)
    st = st * jnp.exp(n * lg_lane) + jnp.where(_head_eq(D_GROUP), kv, 0.0)
    return o, st


def _ret_kernel(lg_ref, qf_ref, kf_ref, vf_ref, cf_ref, sf_ref, qb_ref, kb_ref, vb_ref, cb_ref, sb_ref,
                lgl_ref, of_ref, ob_ref, st_ref):
    @pl.when(pl.program_id(1) == 0)
    def _():
        st_ref[...] = jnp.zeros_like(st_ref)

    o, st = _ret_dir(qf_ref[0], kf_ref[0], vf_ref[0], cf_ref[...], sf_ref[...], lg_ref, 0,
                     lgl_ref[0:1, :], st_ref[0], False)
    of_ref[0] = o
    st_ref[0] = st
    o, st = _ret_dir(qb_ref[0], kb_ref[0], vb_ref[0], cb_ref[...], sb_ref[...], lg_ref, 1,
                     lgl_ref[1:2, :], st_ref[1], True)
    ob_ref[0] = o
    st_ref[1] = st


def _ret_scan(zr, cos_t, sin_t, lg, nbc):
    b, l, _ = zr.shape
    nb = l // BLK
    fwd, bwd = _scan_specs(nbc, nb, (0, 1, 2), (0, 1, 2))
    outs = _scan_specs(nbc, nb, (0,), (0,))
    tab_f = pl.BlockSpec((BLK, D_GROUP), lambda i, s: (s, 0))
    tab_b = pl.BlockSpec((BLK, D_GROUP), lambda i, s: (_bwd_block(s, nbc, nb), 0))
    lg_lane = jnp.repeat(lg, HEAD_DIM, axis=-1)
    return pl.pallas_call(
        _ret_kernel,
        grid=(b, nb),
        in_specs=[pl.BlockSpec(memory_space=pltpu.SMEM)] + fwd + [tab_f, tab_f] + bwd + [tab_b, tab_b]
                 + [pl.BlockSpec((2, D_GROUP), lambda i, s: (0, 0))],
        out_specs=outs[0] + outs[1],
        out_shape=[jax.ShapeDtypeStruct((b, l, D_GROUP), F32)] * 2,
        scratch_shapes=[pltpu.VMEM((2, D_GROUP, D_GROUP), F32)],
        compiler_params=_cparams(("arbitrary", "arbitrary")),
        name="ret_scan",
    )(lg, zr, zr, zr, cos_t, sin_t, zr, zr, zr, cos_t, sin_t, lg_lane)


def _gdn_prep_kernel(prev_ref, cur_ref, next_ref, ab_ref, cw_ref, an_ref, dtb_ref, qkv_ref, ga_ref, xs_ref,
                     *, nbc, nb):
    j = pl.program_id(1)
    is_ctx = j < nbc
    prev_ok = jnp.where(is_ctx, j > 0, j > nbc)
    next_ok = jnp.where(is_ctx, j < nbc - 1, j < nb - 1)
    n = BLK
    xs_ref[0:n, :] = jnp.where(prev_ok, prev_ref[0], 0.0)
    xs_ref[n:2 * n, :] = cur_ref[0]
    xs_ref[2 * n:3 * n, :] = jnp.where(next_ok, next_ref[0], 0.0)
    colpos = lax.broadcasted_iota(jnp.int32, (n, 1), 0) & (GRID_W - 1)
    acc = jnp.zeros((n, 3 * D_GROUP), F32)
    for dr in (-1, 0, 1):
        for dc in (-1, 0, 1):
            win = xs_ref[pl.ds(n + GRID_W * dr + dc, n), :]
            col_ok = (colpos >= 1) if dc == -1 else ((colpos <= GRID_W - 2) if dc == 1 else (colpos >= 0))
            ok = (is_ctx | col_ok) if dr == 0 else (jnp.logical_not(is_ctx) & col_ok)
            acc = acc + jnp.where(ok, win, 0.0) * cw_ref[(dr + 1) * 3 + (dc + 1):(dr + 1) * 3 + (dc + 2), :]
    xc = _silu(acc)
    e = _head_eq(D_GROUP).astype(BF16)
    q = xc[:, 0:D_GROUP]
    k = xc[:, D_GROUP:2 * D_GROUP]
    qkv_ref[0, :, 0:D_GROUP] = q * lax.rsqrt(_seg_sum(q * q, e) + EPS) * HEAD_DIM ** -0.5
    qkv_ref[0, :, D_GROUP:2 * D_GROUP] = k * lax.rsqrt(_seg_sum(k * k, e) + EPS)
    qkv_ref[0, :, 2 * D_GROUP:] = xc[:, 2 * D_GROUP:]
    ab = ab_ref[0]
    z = ab + dtb_ref[...]
    softplus = jnp.maximum(z, 0.0) + jnp.log(1.0 + jnp.exp(-jnp.abs(z)))
    lane = lax.broadcasted_iota(jnp.int32, ab.shape, 1)
    ga_ref[0] = jnp.where(lane < 2 * N_HEADS, an_ref[...] * softplus, jax.nn.sigmoid(ab))


def _gdn_prep(zg, conv_w, a_neg, dt_bias, nbc):
    b, l, _ = zg.shape
    nb = l // BLK
    w3 = 3 * D_GROUP
    return pl.pallas_call(
        functools.partial(_gdn_prep_kernel, nbc=nbc, nb=nb),
        grid=(b, nb),
        in_specs=[pl.BlockSpec((1, BLK, w3), lambda i, j: (i, jnp.maximum(j - 1, 0), 0)),
                  pl.BlockSpec((1, BLK, w3), lambda i, j: (i, j, 0)),
                  pl.BlockSpec((1, BLK, w3), lambda i, j: (i, jnp.minimum(j + 1, nb - 1), 0)),
                  pl.BlockSpec((1, BLK, 128), lambda i, j: (i, j, (ZG_W - 128) // 128)),
                  pl.BlockSpec((9, w3), lambda i, j: (0, 0)),
                  pl.BlockSpec((1, 128), lambda i, j: (0, 0)),
                  pl.BlockSpec((1, 128), lambda i, j: (0, 0))],
        out_specs=[pl.BlockSpec((1, BLK, w3), lambda i, j: (i, j, 0)),
                   pl.BlockSpec((1, BLK, 128), lambda i, j: (i, j, 0))],
        out_shape=[jax.ShapeDtypeStruct((b, l, w3), F32), jax.ShapeDtypeStruct((b, l, 128), F32)],
        scratch_shapes=[pltpu.VMEM((3 * BLK, w3), F32)],
        compiler_params=_cparams(("arbitrary", "arbitrary")),
        name="gdn_prep",
    )(zg, zg, zg, zg, conv_w, a_neg, dt_bias)


def _heads(x):
    return jnp.concatenate([x[:, h * HEAD_DIM:(h + 1) * HEAD_DIM][None] for h in range(N_HEADS)], axis=0)


def _col_heads(x, lane0):
    return jnp.concatenate([jnp.broadcast_to(x[:, lane0 + h:lane0 + h + 1], (GDN_CHUNK, HEAD_DIM))[None]
                            for h in range(N_HEADS)], axis=0)


def _row_heads(xt, row0):
    return jnp.concatenate([jnp.broadcast_to(xt[row0 + h:row0 + h + 1, :], (HEAD_DIM, GDN_CHUNK))[None]
                            for h in range(N_HEADS)], axis=0)


def _bmm16(a, b):
    return _bmm(a.astype(BF16), b.astype(BF16))


def _gdn_gates(ga, rev):
    pos = lax.broadcasted_iota(jnp.int32, ga.shape, 0) & (GDN_CHUNK - 1)
    g2 = _chunk_cumsum(ga, pos, GDN_CHUNK, rev)
    return g2, g2.T, _chunk_total(ga, pos, GDN_CHUNK)


def _gdn_load(group):
    qkv_ref, ga_ref, (g2, g2t, tot2), cidx, d = group
    c = GDN_CHUNK
    rows = slice(cidx * c, (cidx + 1) * c)
    a0 = d * N_HEADS
    b0 = 2 * N_HEADS + d * N_HEADS
    return dict(gc=_col_heads(g2[rows], a0), gr=_row_heads(g2t[:, rows], a0), totc=_col_heads(tot2[rows], a0),
                beta=_col_heads(ga_ref[0, rows, :], b0), q=_heads(qkv_ref[0, rows, 0:D_GROUP]),
                k=_heads(qkv_ref[0, rows, D_GROUP:2 * D_GROUP]), v=_heads(qkv_ref[0, rows, 2 * D_GROUP:]))


def _gdn_par(groups, n_fwd, out):
    c = GDN_CHUNK
    ops = [_gdn_load(g) for g in groups]
    cat = lambda name: jnp.concatenate([o[name] for o in ops], axis=0)
    gc, gr, totc, beta, q3, k3, v3 = (cat(nm) for nm in ('gc', 'gr', 'totc', 'beta', 'q', 'k', 'v'))
    n_inst = q3.shape[0]
    shape = (n_inst, c, c)
    rev = lax.broadcasted_iota(jnp.int32, shape, 0) >= n_fwd * N_HEADS
    ri = lax.broadcasted_iota(jnp.int32, shape, 1)
    ci = lax.broadcasted_iota(jnp.int32, shape, 2)
    lag = jnp.where(rev, ci - ri, ri - ci)
    tri = lag >= 0
    strict = lag > 0
    same_sub = lax.shift_right_logical(ri, 4) == lax.shift_right_logical(ci, 4)
    eye = (ri == ci).astype(F32)

    lmat = jnp.where(tri, jnp.exp(jnp.minimum(gc - gr, 0.0)), 0.0)
    kb = k3 * beta
    k16 = k3.astype(BF16)
    kkt = _bmm_nt(kb.astype(BF16), k16)
    qkt = _bmm_nt(q3.astype(BF16), k16)
    yield
    amat = jnp.where(strict, kkt * lmat, 0.0)
    dmat = jnp.where(same_sub, amat, 0.0)
    lo = amat - dmat
    d2 = _bmm16(dmat, dmat)
    yield
    d4 = _bmm16(d2, d2)
    p1 = _bmm16(eye - dmat, eye + d2)
    yield
    d8 = _bmm16(d4, d4)
    yield
    p2 = _bmm16(eye + d4, eye + d8)
    yield
    tdiag = _bmm16(p1, p2)
    yield
    mmat = _bmm16(tdiag, lo)
    yield
    m2 = _bmm16(mmat, mmat)
    yield
    p3 = _bmm16(eye - mmat, eye + m2)
    yield
    tinv = _bmm16(p3, tdiag).astype(BF16)
    yield
    out.update(u=_bmm(tinv, (v3 * beta).astype(BF16)),
               w=_bmm(tinv, (kb * jnp.exp(gc)).astype(BF16)).astype(BF16),
               qk=jnp.where(tri, qkt * lmat, 0.0).astype(BF16), qd=(q3 * jnp.exp(gc)).astype(BF16),
               kd=(k3 * jnp.exp(totc - gc)).astype(BF16), cdec=jnp.exp(totc))


def _gdn_seq(p, st_box, n_steps, write):
    nh = N_HEADS
    for k in range(n_steps):
        sel = lambda a: jnp.concatenate([a[nh * k:nh * (k + 1)], a[nh * (n_steps + k):nh * (n_steps + k + 1)]], axis=0)
        st = st_box[0]
        s16 = st.astype(BF16)
        ws = _bmm(sel(p['w']), s16)
        qs = _bmm(sel(p['qd']), s16)
        yield
        v16 = (sel(p['u']) - ws).astype(BF16)
        o = qs + _bmm(sel(p['qk']), v16)
        st_box[0] = st * sel(p['cdec']) + jnp.einsum('gck,gcv->gkv', sel(p['kd']), v16, preferred_element_type=F32)
        write(k, jnp.concatenate([o[h] for h in range(nh)], axis=-1),
              jnp.concatenate([o[nh + h] for h in range(nh)], axis=-1))
        yield


def _interleave(*gens):
    gens = list(gens)
    while gens:
        for g in list(gens):
            try:
                next(g)
            except StopIteration:
                gens.remove(g)


def _gdn_kernel(xf_ref, gf_ref, xb_ref, gb_ref, of_ref, ob_ref, st_ref):
    @pl.when(pl.program_id(1) == 0)
    def _():
        st_ref[...] = jnp.zeros_like(st_ref)

    c = GDN_CHUNK
    nc = xf_ref.shape[1] // c
    half = nc // 2
    gates_f = _gdn_gates(gf_ref[0], False)
    gates_b = _gdn_gates(gb_ref[0], True)
    fwd = lambda ci: (xf_ref, gf_ref, gates_f, ci, 0)
    bwd = lambda ci: (xb_ref, gb_ref, gates_b, ci, 1)

    def writer(first_f, first_b):
        def write(k, o_f, o_b):
            cf, cb = first_f + k, first_b - k
            of_ref[0, cf * c:(cf + 1) * c, :] = o_f
            ob_ref[0, cb * c:(cb + 1) * c, :] = o_b
        return write

    st_box = [jnp.concatenate([st_ref[0], st_ref[1]], axis=0)]
    p1, p2 = {}, {}
    _interleave(_gdn_par([fwd(t) for t in range(half)] + [bwd(nc - 1 - t) for t in range(half)], half, p1))
    _interleave(_gdn_par([fwd(half + t) for t in range(half)] + [bwd(nc - 1 - half - t) for t in range(half)],
                         half, p2),
                _gdn_seq(p1, st_box, half, writer(0, nc - 1)))
    _interleave(_gdn_seq(p2, st_box, half, writer(half, nc - 1 - half)))
    st_ref[0] = st_box[0][:N_HEADS]
    st_ref[1] = st_box[0][N_HEADS:]


def _gdn_scan(qkv, ga, nbc):
    b, l, w3 = qkv.shape
    nb = l // BLK
    xf, xb = _scan_specs(nbc, nb, (0,), (0,), w3)
    gf, gb = _scan_specs(nbc, nb, (0,), (0,), 128)
    outs = _scan_specs(nbc, nb, (0,), (0,))
    return pl.pallas_call(
        _gdn_kernel,
        grid=(b, nb),
        in_specs=xf + gf + xb + gb,
        out_specs=outs[0] + outs[1],
        out_shape=[jax.ShapeDtypeStruct((b, l, D_GROUP), F32)] * 2,
        scratch_shapes=[pltpu.VMEM((2, N_HEADS, HEAD_DIM, HEAD_DIM), F32)],
        compiler_params=_cparams(("arbitrary", "arbitrary")),
        name="gdn_scan",
    )(qkv, ga, qkv, ga)


def _cmul(ar, ai, hr, hi):
    return ar * hr - ai * hi, ar * hi + ai * hr


def _s5_pack(re, im):
    lead = re.shape[:-1]
    parts = jnp.stack([re.reshape(lead + (-1, S5_PART)), im.reshape(lead + (-1, S5_PART))], axis=-2)
    return parts.reshape(lead + (2 * S5_LANES,))


def _s5_kernel(u_ref, ws_ref, wc_ref, wt_ref, aux_ref, pw_ref, y_ref, h_ref, *, rev, parts):
    u = u_ref[0]
    rows, width = h_ref.shape
    pw = S5_PART
    pieces = [(slice(o, o + pw), slice(o + pw, o + 2 * pw)) for o in range(0, width, 2 * pw)]
    pos = lax.broadcasted_iota(jnp.int32, (rows // 8, 8, pw), 1)
    gw = pw // S5_STATE * S5_GROUP
    u_q = [jnp.concatenate([u[:, i * D_GROUP + q * gw:i * D_GROUP + (q + 1) * gw] for i in range(S5_CHUNK)], axis=1)
           for q in range(len(pieces))]
    for q, (re, im) in enumerate(pieces):
        x = _dot(u_q[q], ws_ref[q])
        xr = x[:, 0:pw].reshape(rows // 8, 8, pw)
        xi = x[:, pw:].reshape(rows // 8, 8, pw)
        for lvl, k in enumerate((1, 2, 4)):
            dr, di = _cmul(aux_ref[lvl:lvl + 1, re], aux_ref[lvl:lvl + 1, im],
                           pltpu.roll(xr, 8 - k if rev else k, 1), pltpu.roll(xi, 8 - k if rev else k, 1))
            valid = (pos < 8 - k) if rev else (pos >= k)
            xr = xr + jnp.where(valid, dr, 0.0)
            xi = xi + jnp.where(valid, di, 0.0)
        h_ref[:, re] = xr.reshape(rows, pw)
        h_ref[:, im] = xi.reshape(rows, pw)

    sub = lax.broadcasted_iota(jnp.int32, (8, pw), 0)
    edge = 0 if rev else 7

    def tile_step(t, carry):
        sl = pl.ds(pl.multiple_of(t * 8, 8), 8)
        out = []
        for (re, im), (cr, ci) in zip(pieces, carry):
            dr, di = _cmul(pw_ref[:, re], pw_ref[:, im], cr, ci)
            fr, fi = h_ref[sl, re] + dr, h_ref[sl, im] + di
            first = sub == (7 if rev else 0)
            h_ref[sl, re] = jnp.where(first, cr, pltpu.roll(fr, 7 if rev else 1, 0))
            h_ref[sl, im] = jnp.where(first, ci, pltpu.roll(fi, 7 if rev else 1, 0))
            out.append((jnp.broadcast_to(fr[edge:edge + 1, :], fr.shape),
                        jnp.broadcast_to(fi[edge:edge + 1, :], fi.shape)))
        return tuple(out)

    carry = tuple((jnp.zeros((8, pw), F32), jnp.zeros((8, pw), F32)) for _ in pieces)
    for row0, nrows in parts:
        t0, nt = row0 // 8, nrows // 8
        if rev:
            carry = lax.fori_loop(0, nt, lambda i, c, t0=t0, nt=nt: tile_step(t0 + nt - 1 - i, c), carry)
        else:
            carry = lax.fori_loop(0, nt, lambda i, c, t0=t0: tile_step(t0 + i, c), carry)
    y_q = [_dot(u_q[q], wt_ref[q]) + _dot(h_ref[:, re.start:im.stop].astype(BF16), wc_ref[q])
           for q, (re, im) in enumerate(pieces)]
    y_pos = [jnp.concatenate([y_q[q][:, i * gw:(i + 1) * gw] for q in range(len(pieces))], axis=1).astype(BF16)
             for i in range(S5_CHUNK)]
    perm = _chunk_select(BLK, True)
    n_chunks = BLK // S5_CHUNK
    for blk in range(rows // n_chunks):
        stacked = jnp.concatenate([y[blk * n_chunks:(blk + 1) * n_chunks] for y in y_pos], axis=0)
        y_ref[0, blk * BLK:(blk + 1) * BLK, :] = _dot(perm, stacked).astype(BF16)


def _s5_scan(u_c, weights, li, dd, parts):
    bsz, rows, w = u_c.shape
    pick = lambda a: pl.BlockSpec((None, None) + a.shape[2:], lambda b: (li, dd) + (0,) * (a.ndim - 2))
    n_tok = rows * S5_CHUNK
    return pl.pallas_call(
        functools.partial(_s5_kernel, rev=dd == 1, parts=parts),
        grid=(bsz,),
        in_specs=[pl.BlockSpec((1, rows, w), lambda b: (b, 0, 0))] + [pick(a) for a in weights],
        out_specs=pl.BlockSpec((1, n_tok, D_GROUP), lambda b: (b, 0, 0)),
        out_shape=jax.ShapeDtypeStruct((bsz, n_tok, D_GROUP), BF16),
        scratch_shapes=[pltpu.VMEM((rows, 2 * S5_LANES), F32)],
        compiler_params=_cparams(("arbitrary",)),
        name="s5_scan_bwd" if dd == 1 else "s5_scan_fwd",
    )(u_c, *weights)


def _s5_weights(lam_re, lam_im, log_dt, b_re, b_im, c_re, c_im):
    cch, ng = S5_CHUNK, S5_GROUPS
    nq = S5_LANES // S5_PART
    gl = ng // nq
    dt = jnp.exp(log_dt)[..., None]
    ang, dec = lam_im * dt, lam_re * dt

    def power(n_fwd, n_bwd=None):
        n = np.stack([n_fwd, n_fwd if n_bwd is None else n_bwd], axis=1).astype(np.float32)
        n = jnp.asarray(n).reshape(-1, 1, 2, 1, 1)
        mag = jnp.exp(n * dec)
        return mag * jnp.cos(n * ang), mag * jnp.sin(n * ang)

    pr, pi = power(np.arange(cch + 1))
    ar, ai = pr[1], pi[1]
    den = lam_re * lam_re + lam_im * lam_im
    nr, ni = ar - 1.0, ai
    fr = (nr * lam_re + ni * lam_im) / den
    fi = (ni * lam_re - nr * lam_im) / den
    bbr = fr[..., None] * b_re[:, None] - fi[..., None] * b_im[:, None]
    bbi = fr[..., None] * b_im[:, None] + fi[..., None] * b_re[:, None]
    eye = jnp.eye(gl, dtype=F32)
    pieces = lambda x, axis: x.reshape(x.shape[:axis] + (nq, gl) + x.shape[axis + 1:])
    idx = np.arange(cch)

    af_r, af_i = power(cch - 1 - idx, idx)
    ws = jnp.stack([af_r[..., None] * bbr - af_i[..., None] * bbi, af_r[..., None] * bbi + af_i[..., None] * bbr], -1)
    ws = jnp.einsum('ndrqgpcs,gh->drqngcshp', pieces(ws, 3), eye)
    ws = ws.reshape(ws.shape[:3] + (cch * gl * S5_GROUP, 2 * S5_PART))
    up_r, up_i = power(idx + 1, cch - idx)
    cr, ci = c_re[:, None], c_im[:, None]
    wc = jnp.stack([cr * up_r[..., None, :] - ci * up_i[..., None, :],
                    -(cr * up_i[..., None, :] + ci * up_r[..., None, :])], -1)
    wc = jnp.einsum('ndrqgcps,gh->drqshpngc', pieces(wc, 3), eye)
    wc = wc.reshape(wc.shape[:3] + (2 * S5_PART, cch * gl * S5_GROUP))
    ca_r = cr * pr[:cch, ..., None, :] - ci * pi[:cch, ..., None, :]
    ca_i = cr * pi[:cch, ..., None, :] + ci * pr[:cch, ..., None, :]
    tap = functools.partial(jnp.einsum, 'ndrgcp,drgpe->ndrgec', precision=lax.Precision.HIGHEST)
    taps = tap(ca_r, bbr) - tap(ca_i, bbi)
    lag = idx[None, :] - idx[:, None]
    tl = jnp.where((lag >= 0).reshape(cch, cch, 1, 1, 1, 1, 1), taps[np.maximum(lag, 0)], 0.0)
    is_bwd = (np.arange(2) == 1).reshape(1, 1, 1, 2, 1, 1, 1)
    wt = jnp.where(is_bwd, jnp.swapaxes(tl, 0, 1), tl)
    wt = jnp.einsum('jidrqgec,gh->drqjgeihc', pieces(wt, 4), eye)
    wt = wt.reshape(wt.shape[:3] + (cch * gl * S5_GROUP, cch * gl * S5_GROUP))

    def table(n_fwd, n_bwd):
        flat = lambda t: t.reshape(t.shape[:3] + (S5_LANES,))
        tr, ti = power(n_fwd, n_bwd)
        return jnp.moveaxis(_s5_pack(flat(tr), flat(ti)), 0, 2)

    doubling = cch * np.array([1, 2, 4, 0, 0, 0, 0, 0])
    aux = table(doubling, doubling)
    pw = table(cch * (np.arange(8) + 1), cch * (8 - np.arange(8)))
    return ws.astype(BF16), wc.astype(BF16), wt.astype(BF16), aux, pw


def _gated_norm(o, gate, gain, e):
    y = o * lax.rsqrt(_seg_sum(o * o, e) * (1.0 / HEAD_DIM) + EPS)
    if gain is not None:
        y = y * gain
    return y * _silu(gate)


def _gelu_tanh(x):
    return 0.5 * x * (1.0 + jnp.tanh(np.sqrt(2.0 / np.pi) * (x + 0.044715 * (x * x * x))))


def _out_mlp_kernel(c_ref, x_ref, mod_ref, haf_ref, hab_ref, hg_ref, rf_ref, rb_ref, rg_ref, gf_ref, gb_ref, gg_ref,
                    sf_ref, sb_ref, su_ref, hng_ref, gng_ref, sd_ref, glw_ref, glb_ref, wo_ref, n2_ref, w1_ref,
                    w2_ref, fg_ref, o_ref, *, final, nbc, off):
    m = mod_ref[0, 0]
    x0 = jnp.where(pl.program_id(1) + off < nbc, c_ref[0], x_ref[0])
    e = _head_eq(D_GROUP).astype(BF16)
    a = _gated_norm(haf_ref[0] + hab_ref[0], hg_ref[0], hng_ref[...], e)
    r = _gated_norm(rf_ref[0] + rb_ref[0], rg_ref[0], None, e)
    g = _gated_norm(gf_ref[0] + gb_ref[0], gg_ref[0], gng_ref[...], e)
    s = _gelu_tanh(sf_ref[0].astype(F32) + sb_ref[0].astype(F32) + su_ref[0] * sd_ref[...])
    s = s * jax.nn.sigmoid(_dot(s, glw_ref[...]) + glb_ref[...])
    y = jnp.zeros_like(x0)
    for i, part in enumerate((a, r, g, s)):
        y = y + _dot(part.astype(BF16), wo_ref[i * D_GROUP:(i + 1) * D_GROUP, :])
    x1 = x0 + m[2:3] * y
    h2 = _rms_mod(x1, n2_ref[...], m[3:4], m[4:5]).astype(BF16)
    acc = jnp.zeros_like(x1)
    for c0 in range(0, D_FF, FF_CHUNK):
        hid = jnp.maximum(_dot(h2, w1_ref[:, c0:c0 + FF_CHUNK]), 0.0)
        acc = acc + _dot((hid * hid).astype(BF16), w2_ref[c0:c0 + FF_CHUNK, :])
    x2 = x1 + m[5:6] * acc
    if final:
        x2 = x2 * lax.rsqrt(jnp.mean(x2 * x2, axis=-1, keepdims=True) + EPS) * fg_ref[...]
    o_ref[0] = x2


def _out_mlp(src, modv, za, zr, zg, zs, ha, ra, ga, sa, hng, gng, sd, glw, glb, wo, n2, w1, w2, fg, nbc, final):
    b, l, _ = za.shape
    d = src[0].shape[-1]
    nb = l // BLK
    off = nbc if final else 0
    rows = lambda width, c=0: pl.BlockSpec((1, BLK, width), lambda i, j: (i, j + off, c))
    const = lambda arr: pl.BlockSpec(arr.shape, lambda i, j: (0,) * arr.ndim)
    seg = lambda i, j: (i, jnp.where(j + off < nbc, 0, 1), 0, 0)
    weights = [hng, gng, sd, glw, glb, wo, n2, w1, w2, fg]
    args = [src[0], src[1], modv, ha[0], ha[1], za, ra[0], ra[1], zr, ga[0], ga[1], zg, sa[0], sa[1], zs] + weights
    in_specs = _seq_specs(src, nbc, off) + [pl.BlockSpec((1, 1, 8, d), seg),
                rows(D_GROUP), rows(D_GROUP), rows(D_GROUP, 2),
                rows(D_GROUP), rows(D_GROUP), rows(D_GROUP, 3),
                rows(D_GROUP), rows(D_GROUP), rows(D_GROUP, 3),
                rows(D_GROUP), rows(D_GROUP), rows(D_GROUP)] + [const(a) for a in weights]
    return pl.pallas_call(
        functools.partial(_out_mlp_kernel, final=final, nbc=nbc, off=off),
        grid=(b, nb - off),
        in_specs=in_specs,
        out_specs=pl.BlockSpec((1, BLK, d), lambda i, j: (i, j, 0)),
        out_shape=jax.ShapeDtypeStruct((b, l - off * BLK, d), F32),
        compiler_params=_cparams(("arbitrary", "arbitrary")),
        name="out_mlp_final" if final else "out_mlp",
    )(*args)


def kernel(x, c, ctx, c_ctx, mod_w, mod_b, norm1_g, norm2_g, w_in, hgrn_lb_logits, hgrn_norm_g, ret_decay_logit,
           gdn_conv_w, gdn_a_log, gdn_dt_bias, gdn_norm_g, s5_lam_re, s5_lam_im, s5_log_dt, s5_b_re, s5_b_im,
           s5_c_re, s5_c_im, s5_d, s5_glu_w, s5_glu_b, w_out, mlp_w1, mlp_w2, final_norm_g):
    bsz, t_lat, d = x.shape
    t_ctx = ctx.shape[1]
    depth = mod_w.shape[0]
    assert d == D_MODEL and t_ctx % BLK == 0 and t_lat % BLK == 0 and t_lat % GRID_W == 0
    assert t_ctx % (8 * S5_CHUNK) == 0 and t_lat % (8 * S5_CHUNK) == 0
    l = t_ctx + t_lat
    nbc = t_ctx // BLK

    src = (ctx.astype(F32), x.astype(F32), nbc)

    n_rows = -(-(bsz + 1) // 8) * 8
    cvecs = jnp.zeros((n_rows, d), F32).at[:bsz].set(c.astype(F32)).at[bsz].set(c_ctx.astype(F32))
    mod = _mod_proj(cvecs, mod_w.astype(F32), mod_b.astype(F32)).reshape(depth, n_rows, N_MOD, d)
    mod_lat = mod[:, :bsz]
    mod_ctx = jnp.broadcast_to(mod[:, bsz:bsz + 1], mod_lat.shape)
    modv = jnp.stack([mod_ctx, mod_lat], axis=2)
    modv = jnp.pad(modv, ((0, 0), (0, 0), (0, 0), (0, 8 - N_MOD), (0, 0)))

    sm = jax.nn.softmax(hgrn_lb_logits.astype(F32), axis=0)
    lbs = jnp.cumsum(sm, axis=0) - sm[:1]
    pos = jnp.arange(l, dtype=F32)
    half = HEAD_DIM // 2
    inv = ROPE_BASE ** (-jnp.arange(half, dtype=F32) / half)
    ang = pos[:, None] * inv[None, :]
    cos_t = jnp.tile(jnp.cos(ang), (1, 2 * N_HEADS))
    sin_t = jnp.tile(jnp.concatenate([-jnp.sin(ang), jnp.sin(ang)], axis=1), (1, N_HEADS))
    n_main = ZA_W + ZR_W + ZG_W - 128
    w_main = w_in.astype(BF16)
    w_ab = jnp.pad(w_main[..., n_main:n_main + 4 * N_HEADS], ((0, 0), (0, 0), (0, 128 - 4 * N_HEADS)))
    w_s = w_main[..., n_main + 4 * N_HEADS:]
    pad8 = lambda v: jnp.pad(v.astype(F32).reshape(1, 2 * N_HEADS), ((0, 0), (0, 128 - 2 * N_HEADS)))
    s5_w = _s5_weights(*(p.astype(F32) for p in (s5_lam_re, s5_lam_im, s5_log_dt, s5_b_re, s5_b_im, s5_c_re, s5_c_im)))
    s5_parts = ((0, t_ctx // S5_CHUNK), (t_ctx // S5_CHUNK, t_lat // S5_CHUNK))

    out = None
    for li in range(depth):
        final = li == depth - 1
        za, zr, zg, zs, u_c = _in_proj(src, l, modv[li], norm1_g[li].astype(F32).reshape(1, d),
                                       w_main[li], w_ab[li], w_s[li], nbc)
        ha = _hgrn_scan(za, lbs[li], nbc)
        lg = jax.nn.log_sigmoid(ret_decay_logit[li].astype(F32))
        ra = _ret_scan(zr, cos_t, sin_t, lg, nbc)
        qkv, gab = _gdn_prep(zg, gdn_conv_w[li].astype(F32).reshape(9, 3 * D_GROUP),
                             pad8(-jnp.exp(gdn_a_log[li].astype(F32))), pad8(gdn_dt_bias[li]), nbc)
        ga = _gdn_scan(qkv, gab, nbc)
        sa = [_s5_scan(u_c, s5_w, li, dd, s5_parts) for dd in range(2)]
        row = lambda v: v.astype(F32).reshape(1, -1)
        res = _out_mlp(src, modv[li], za, zr, zg, zs, ha, ra, ga, sa,
                       row(jnp.tile(hgrn_norm_g[li], N_HEADS)), row(jnp.tile(gdn_norm_g[li], N_HEADS)),
                       row(s5_d[li]), s5_glu_w[li].astype(F32), row(s5_glu_b[li]),
                       w_out[li].astype(BF16), row(norm2_g[li]), mlp_w1[li].astype(BF16), mlp_w2[li].astype(BF16),
                       row(final_norm_g), nbc, final)
        if final:
            out = res
        else:
            src = (res, res, 0)
    return out.astype(x.dtype)
```

```python
import functools

import jax
import jax.numpy as jnp
import numpy as np
from jax import lax
from jax.experimental import pallas as pl
from jax.experimental.pallas import tpu as pltpu

F32 = jnp.float32
BF16 = jnp.bfloat16

D_MODEL = 1024
D_GROUP = D_MODEL // 4
N_HEADS = 4
HEAD_DIM = D_GROUP // N_HEADS
D_FF = 4 * D_MODEL
N_MOD = 6
EPS = 1e-6
LB_FLOOR = 1e-30
LOG2E = 1.4426950408889634
GRID_W = 64
ROPE_BASE = 10000.0
S5_GROUP = 16
S5_GROUPS = D_GROUP // S5_GROUP
S5_STATE = 64
S5_LANES = S5_GROUPS * S5_STATE

BLK = 256
HG_CHUNK = 16
GDN_CHUNK = 64
GDN_SUB = 16
GDN_HALO = 128
GDN_BATCH = 4
S5_CHUNK = 4
S5_PART = 256
FF_CHUNK = 1024
V7X_VMEM_LIMIT = 56 * 1024 * 1024

ZA_W = 5 * D_GROUP
ZR_W = 4 * D_GROUP
ZG_W = 4 * D_GROUP + 128
ZS_W = D_GROUP


def _cparams(sem):
    return pltpu.CompilerParams(dimension_semantics=sem, vmem_limit_bytes=V7X_VMEM_LIMIT)


def _bwd_block(s, nbc, nb):
    return jnp.where(s < nbc, nbc - 1 - s, nb - 1 - (s - nbc))


def _dot(a, b):
    return jnp.dot(a, b, preferred_element_type=F32)


def _dot_nt(a, b):
    return lax.dot_general(a, b, (((1,), (1,)), ((), ())), preferred_element_type=F32)


def _dot_tn(a, b):
    return lax.dot_general(a, b, (((0,), (0,)), ((), ())), preferred_element_type=F32)


def _bmm(a, b):
    return jnp.einsum('gij,gjk->gik', a, b, preferred_element_type=F32)


def _bmm_nt(a, b):
    return jnp.einsum('gik,gjk->gij', a, b, preferred_element_type=F32)


def _silu(x):
    return x * jax.nn.sigmoid(x)


def _head_eq(n):
    r = lax.shift_right_logical(lax.broadcasted_iota(jnp.int32, (n, n), 0), 6)
    c = lax.shift_right_logical(lax.broadcasted_iota(jnp.int32, (n, n), 1), 6)
    return r == c


def _seg_sum(x, e):
    hi = x.astype(BF16)
    lo = (x - hi.astype(F32)).astype(BF16)
    return _dot(hi, e) + _dot(lo, e)


def _rms_mod(x, g, shift, scale):
    h = x * lax.rsqrt(jnp.mean(x * x, axis=-1, keepdims=True) + EPS) * g
    return h * (1.0 + scale) + shift


def _chunk_cumsum(x, pos, c, rev):
    n = x.shape[0]
    sh = 1
    while sh < c:
        if rev:
            x = x + jnp.where(pos < c - sh, pltpu.roll(x, n - sh, 0), 0.0)
        else:
            x = x + jnp.where(pos >= sh, pltpu.roll(x, sh, 0), 0.0)
        sh *= 2
    return x


def _chunk_total(x, pos, c):
    n = x.shape[0]
    sh = 1
    while sh < c:
        x = x + jnp.where(pos >= sh, pltpu.roll(x, sh, 0), pltpu.roll(x, n - (c - sh), 0))
        sh *= 2
    return x


def _mod_kernel(c_ref, w_ref, b_ref, o_ref):
    o_ref[0] = _dot(_silu(c_ref[...]), w_ref[0]) + b_ref[0]


def _mod_proj(cvecs, mod_w, mod_b):
    depth, d, n = mod_w.shape
    rows = cvecs.shape[0]
    tn = 1536
    return pl.pallas_call(
        _mod_kernel,
        grid=(depth, n // tn),
        in_specs=[pl.BlockSpec((rows, d), lambda l, j: (0, 0)),
                  pl.BlockSpec((1, d, tn), lambda l, j: (l, 0, j)),
                  pl.BlockSpec((1, 1, tn), lambda l, j: (l, 0, j))],
        out_specs=pl.BlockSpec((1, rows, tn), lambda l, j: (l, 0, j)),
        out_shape=jax.ShapeDtypeStruct((depth, rows, n), F32),
        compiler_params=_cparams(("arbitrary", "arbitrary")),
        name="mod_proj",
    )(cvecs, mod_w, mod_b.reshape(depth, 1, n))


def _chunk_select(n_tok, tokens_on_rows):
    a = lax.broadcasted_iota(jnp.int32, (n_tok, n_tok), 1 if tokens_on_rows else 0)
    t = lax.broadcasted_iota(jnp.int32, (n_tok, n_tok), 0 if tokens_on_rows else 1)
    n_chunks = n_tok // S5_CHUNK
    shift = n_chunks.bit_length() - 1
    assert n_chunks == 1 << shift
    return (t == (a & (n_chunks - 1)) * S5_CHUNK + lax.shift_right_logical(a, shift)).astype(BF16)


def _seq_specs(src, nbc, off):
    ctx_arr, lat_arr, shift = src
    d = ctx_arr.shape[-1]
    return [pl.BlockSpec((1, BLK, d), lambda i, j: (i, jnp.minimum(j + off, nbc - 1), 0)),
            pl.BlockSpec((1, BLK, d), lambda i, j: (i, jnp.maximum(j + off - shift, 0), 0))]


def _in_proj_kernel(c_ref, x_ref, mod_ref, g_ref, w_ref, wab_ref, ws_ref, za_ref, zr_ref, zg_ref, zs_ref, zc_ref, *,
                    nbc):
    m = mod_ref[0, 0]
    x = jnp.where(pl.program_id(1) < nbc, c_ref[0], x_ref[0])
    h = _rms_mod(x, g_ref[...], m[0:1], m[1:2]).astype(BF16)
    za_ref[0] = _dot(h, w_ref[:, 0:ZA_W])
    zr_ref[0] = _dot(h, w_ref[:, ZA_W:ZA_W + ZR_W])
    zg_ref[0, :, 0:ZG_W - 128] = _dot(h, w_ref[:, ZA_W + ZR_W:ZA_W + ZR_W + ZG_W - 128])
    zg_ref[0, :, ZG_W - 128:] = _dot(h, wab_ref[...])
    zs_ref[0] = _dot(h, ws_ref[...])
    zp = _dot(_chunk_select(BLK, False), zs_ref[0].astype(BF16)).astype(BF16)
    n_chunks = BLK // S5_CHUNK
    for pos in range(S5_CHUNK):
        zc_ref[0, :, pos * ZS_W:(pos + 1) * ZS_W] = zp[pos * n_chunks:(pos + 1) * n_chunks, :]


def _in_proj(src, l, modv, g, w, w_ab, w_s, nbc):
    b, _, d = src[0].shape
    nb = l // BLK
    row = lambda width: pl.BlockSpec((1, BLK, width), lambda i, j: (i, j, 0))
    const = lambda a: pl.BlockSpec(a.shape, lambda i, j: (0,) * a.ndim)
    return pl.pallas_call(
        functools.partial(_in_proj_kernel, nbc=nbc),
        grid=(b, nb),
        in_specs=_seq_specs(src, nbc, 0) + [
                  pl.BlockSpec((1, 1, 8, d), lambda i, j: (i, jnp.where(j < nbc, 0, 1), 0, 0)),
                  pl.BlockSpec((1, d), lambda i, j: (0, 0)), const(w), const(w_ab), const(w_s)],
        out_specs=[row(ZA_W), row(ZR_W), row(ZG_W), row(ZS_W),
                   pl.BlockSpec((1, BLK // S5_CHUNK, S5_CHUNK * ZS_W), lambda i, j: (i, j, 0))],
        out_shape=[jax.ShapeDtypeStruct((b, l, width), F32) for width in (ZA_W, ZR_W, ZG_W, ZS_W)]
                  + [jax.ShapeDtypeStruct((b, l // S5_CHUNK, S5_CHUNK * ZS_W), BF16)],
        compiler_params=_cparams(("arbitrary", "arbitrary")),
        name="in_proj",
    )(src[0], src[1], modv, g, w, w_ab, w_s)


def _hgrn_dir(q_raw, v, f_raw, lb, st, rev):
    n = q_raw.shape[0]
    c = HG_CHUNK
    pos = lax.broadcasted_iota(jnp.int32, (n, D_GROUP), 0) & (c - 1)
    heq = _head_eq(D_GROUP)
    e = heq.astype(BF16)
    qs = _silu(q_raw) * HEAD_DIM ** -0.5
    lbm = jnp.maximum(lb, LB_FLOOR)
    ex = jnp.exp(-jnp.abs(f_raw))
    inv = 1.0 / (1.0 + ex)
    sig_pos = jnp.where(f_raw >= 0, inv, ex * inv)
    sig_neg = jnp.where(f_raw >= 0, ex * inv, inv)
    logf = jnp.log(lbm * sig_neg + sig_pos) * LOG2E
    kk = (1.0 - lbm) * sig_neg
    bcum = _chunk_cumsum(logf, pos, c, rev)
    tot = _chunk_total(logf, pos, c)
    lk = jnp.log(kk) * LOG2E
    bk = bcum - lk

    nc, hc = n // c, c // 2
    split = lambda x: x.reshape(nc, 2, hc, D_GROUP)
    half = lambda x4, h: x4[:, h].reshape(nc * hc, D_GROUP)
    b4, bk4, lk4, v4, q4 = split(bcum), split(bk), split(lk), split(v), split(qs)
    bh = [half(b4, 0), half(b4, 1)]
    qh = [half(q4, 0), half(q4, 1)]
    pos8 = lax.broadcasted_iota(jnp.int32, (nc * hc, D_GROUP), 0) & (hc - 1)
    acc = [jnp.zeros((nc * hc, D_GROUP), F32), jnp.zeros((nc * hc, D_GROUP), F32)]
    qd = qs * jnp.exp2(bcum)
    kd = (kk * jnp.exp2(tot - bcum)).astype(BF16)
    vt = v.T.astype(BF16)
    cdec = jnp.exp2(tot)
    zeros = lambda r: jnp.zeros((r, D_GROUP), BF16)
    outs = [None] * nc
    for j in range(c):
        hj, jj = j // hc, j % hc
        row = lambda x4: jnp.broadcast_to(x4[:, hj, jj:jj + 1, :], (nc, hc, D_GROUP)).reshape(nc * hc, D_GROUP)
        bkj, lkj, vj = row(bk4), row(lk4), row(v4)
        other = 0 if rev else 1
        for ht in ((hj,) if hj == other else (hj, other)):
            term = qh[ht] * jnp.exp2(jnp.minimum(bh[ht] - bkj, lkj))
            contrib = _dot(term.astype(BF16), e) * vj
            if ht == hj:
                contrib = jnp.where((pos8 <= jj) if rev else (pos8 >= jj), contrib, 0.0)
            acc[ht] = acc[ht] + contrib

        ci = nc - 1 - j if rev else j
        rows = slice(ci * c, (ci + 1) * c)
        outs[ci] = _dot_nt(qd[rows], st)
        pieces = [zeros(ci * c), kd[rows], zeros(n - (ci + 1) * c)]
        kd_n = jnp.concatenate([p for p in pieces if p.shape[0]], axis=0)
        st = st * cdec[ci * c:ci * c + 1, :] + jnp.where(heq, _dot(vt, kd_n), 0.0)
    o = jnp.concatenate([acc[0].reshape(nc, 1, hc, D_GROUP), acc[1].reshape(nc, 1, hc, D_GROUP)],
                        axis=1).reshape(n, D_GROUP)
    return o + jnp.concatenate(outs, axis=0), st


def _hgrn_kernel(qf_ref, vf_ref, ff_ref, qb_ref, vb_ref, fb_ref, lb_ref, of_ref, ob_ref, st_ref):
    @pl.when(pl.program_id(1) == 0)
    def _():
        st_ref[...] = jnp.zeros_like(st_ref)

    o, st = _hgrn_dir(qf_ref[0], vf_ref[0], ff_ref[0], lb_ref[0:1, :], st_ref[0], False)
    of_ref[0] = o
    st_ref[0] = st
    o, st = _hgrn_dir(qb_ref[0], vb_ref[0], fb_ref[0], lb_ref[1:2, :], st_ref[1], True)
    ob_ref[0] = o
    st_ref[1] = st


def _scan_specs(nbc, nb, cols_f, cols_b, width=D_GROUP):
    fwd = [pl.BlockSpec((1, BLK, width), functools.partial(lambda i, s, c: (i, s, c), c=c)) for c in cols_f]
    bwd = [pl.BlockSpec((1, BLK, width), functools.partial(lambda i, s, c: (i, _bwd_block(s, nbc, nb), c), c=c))
           for c in cols_b]
    return fwd, bwd


def _hgrn_scan(za, lb, nbc):
    b, l, _ = za.shape
    nb = l // BLK
    fwd, bwd = _scan_specs(nbc, nb, (0, 1, 3), (0, 1, 4))
    outs = _scan_specs(nbc, nb, (0,), (0,))
    return pl.pallas_call(
        _hgrn_kernel,
        grid=(b, nb),
        in_specs=fwd + bwd + [pl.BlockSpec((2, D_GROUP), lambda i, s: (0, 0))],
        out_specs=outs[0] + outs[1],
        out_shape=[jax.ShapeDtypeStruct((b, l, D_GROUP), F32)] * 2,
        scratch_shapes=[pltpu.VMEM((2, D_GROUP, D_GROUP), F32)],
        compiler_params=_cparams(("arbitrary", "arbitrary")),
        name="hgrn_scan",
    )(za, za, za, za, za, za, lb)


def _rope(x, cos, sin_signed):
    lane = lax.broadcasted_iota(jnp.int32, x.shape, 1) & (HEAD_DIM - 1)
    half = HEAD_DIM // 2
    partner = jnp.where(lane < half, pltpu.roll(x, D_GROUP - half, 1), pltpu.roll(x, half, 1))
    return x * cos + partner * sin_signed


def _ret_dir(q, k, v, cos, sin, lg_ref, d, lg_lane, st, rev):
    n = q.shape[0]
    q = _rope(q, cos, sin)
    k = _rope(k, cos, sin) * HEAD_DIM ** -0.5
    ri = lax.broadcasted_iota(jnp.int32, (n, n), 0)
    ci = lax.broadcasted_iota(jnp.int32, (n, n), 1)
    rel = (ci - ri) if rev else (ri - ci)
    relf = jnp.maximum(rel, 0).astype(F32)
    t = lax.broadcasted_iota(jnp.int32, (n, D_GROUP), 0).astype(F32)
    lane_head = lax.shift_right_logical(lax.broadcasted_iota(jnp.int32, (n, D_GROUP), 1), 6)
    if rev:
        qdec, kdec = jnp.exp((n - t) * lg_lane), jnp.exp(t * lg_lane)
    else:
        qdec, kdec = jnp.exp((t + 1.0) * lg_lane), jnp.exp((n - 1.0 - t) * lg_lane)
    kb = k.astype(BF16)
    vb = v.astype(BF16)
    o = _dot((q * qdec).astype(BF16), st.astype(BF16))
    q_heads = jnp.concatenate([jnp.where(lane_head == h, q, 0.0).astype(BF16) for h in range(N_HEADS)], axis=0)
    dmats = jnp.concatenate([jnp.where(rel >= 0, jnp.exp(relf * lg_ref[d, h]), 0.0) for h in range(N_HEADS)], axis=0)
    o_heads = _dot((_dot_nt(q_heads, kb) * dmats).astype(BF16), vb)
    for h in range(N_HEADS):
        o = o + jnp.where(lane_head == h, o_heads[h * n:(h + 1) * n], 0.0)
    kv = _dot((k * kdec).T.astype(BF16), vb)
    st = st * jnp.exp(n * lg_lane) + jnp.where(_head_eq(D_GROUP), kv, 0.0)
    return o, st


def _ret_kernel(lg_ref, qf_ref, kf_ref, vf_ref, cf_ref, sf_ref, qb_ref, kb_ref, vb_ref, cb_ref, sb_ref,
                lgl_ref, of_ref, ob_ref, st_ref):
    @pl.when(pl.program_id(1) == 0)
    def _():
        st_ref[...] = jnp.zeros_like(st_ref)

    o, st = _ret_dir(qf_ref[0], kf_ref[0], vf_ref[0], cf_ref[...], sf_ref[...], lg_ref, 0,
                     lgl_ref[0:1, :], st_ref[0], False)
    of_ref[0] = o
    st_ref[0] = st
    o, st = _ret_dir(qb_ref[0], kb_ref[0], vb_ref[0], cb_ref[...], sb_ref[...], lg_ref, 1,
                     lgl_ref[1:2, :], st_ref[1], True)
    ob_ref[0] = o
    st_ref[1] = st


def _ret_scan(zr, cos_t, sin_t, lg, nbc):
    b, l, _ = zr.shape
    nb = l // BLK
    fwd, bwd = _scan_specs(nbc, nb, (0, 1, 2), (0, 1, 2))
    outs = _scan_specs(nbc, nb, (0,), (0,))
    tab_f = pl.BlockSpec((BLK, D_GROUP), lambda i, s: (s, 0))
    tab_b = pl.BlockSpec((BLK, D_GROUP), lambda i, s: (_bwd_block(s, nbc, nb), 0))
    lg_lane = jnp.repeat(lg, HEAD_DIM, axis=-1)
    return pl.pallas_call(
        _ret_kernel,
        grid=(b, nb),
        in_specs=[pl.BlockSpec(memory_space=pltpu.SMEM)] + fwd + [tab_f, tab_f] + bwd + [tab_b, tab_b]
                 + [pl.BlockSpec((2, D_GROUP), lambda i, s: (0, 0))],
        out_specs=outs[0] + outs[1],
        out_shape=[jax.ShapeDtypeStruct((b, l, D_GROUP), F32)] * 2,
        scratch_shapes=[pltpu.VMEM((2, D_GROUP, D_GROUP), F32)],
        compiler_params=_cparams(("arbitrary", "arbitrary")),
        name="ret_scan",
    )(lg, zr, zr, zr, cos_t, sin_t, zr, zr, zr, cos_t, sin_t, lg_lane)


def _gdn_prep_kernel(prev_ref, cur_ref, next_ref, ab_ref, cw_ref, an_ref, dtb_ref, qkv_ref, ga_ref, xs_ref,
                     *, nbc, nb):
    j = pl.program_id(1)
    is_ctx = j < nbc
    prev_ok = jnp.where(is_ctx, j > 0, j > nbc)
    next_ok = jnp.where(is_ctx, j < nbc - 1, j < nb - 1)
    n, halo = BLK, GDN_HALO
    xs_ref[0:halo, :] = jnp.where(prev_ok, prev_ref[0], 0.0)
    xs_ref[halo:halo + n, :] = cur_ref[0]
    xs_ref[halo + n:, :] = jnp.where(next_ok, next_ref[0], 0.0)
    colpos = lax.broadcasted_iota(jnp.int32, (n, 1), 0) & (GRID_W - 1)
    acc = jnp.zeros((n, 3 * D_GROUP), F32)
    for dr in (-1, 0, 1):
        for dc in (-1, 0, 1):
            win = xs_ref[pl.ds(halo + GRID_W * dr + dc, n), :]
            col_ok = (colpos >= 1) if dc == -1 else ((colpos <= GRID_W - 2) if dc == 1 else (colpos >= 0))
            ok = (is_ctx | col_ok) if dr == 0 else (jnp.logical_not(is_ctx) & col_ok)
            acc = acc + jnp.where(ok, win, 0.0) * cw_ref[(dr + 1) * 3 + (dc + 1):(dr + 1) * 3 + (dc + 2), :]
    xc = _silu(acc)
    e = _head_eq(D_GROUP).astype(BF16)
    q = xc[:, 0:D_GROUP]
    k = xc[:, D_GROUP:2 * D_GROUP]
    qkv_ref[0, :, 0:D_GROUP] = q * lax.rsqrt(_seg_sum(q * q, e) + EPS) * HEAD_DIM ** -0.5
    qkv_ref[0, :, D_GROUP:2 * D_GROUP] = k * lax.rsqrt(_seg_sum(k * k, e) + EPS)
    qkv_ref[0, :, 2 * D_GROUP:] = xc[:, 2 * D_GROUP:]
    ab = ab_ref[0]
    z = ab + dtb_ref[...]
    softplus = jnp.maximum(z, 0.0) + jnp.log(1.0 + jnp.exp(-jnp.abs(z)))
    lane = lax.broadcasted_iota(jnp.int32, ab.shape, 1)
    ga_ref[0] = jnp.where(lane < 2 * N_HEADS, an_ref[...] * softplus, jax.nn.sigmoid(ab))


def _gdn_prep(zg, conv_w, a_neg, dt_bias, nbc):
    b, l, _ = zg.shape
    nb = l // BLK
    w3 = 3 * D_GROUP
    per = BLK // GDN_HALO
    return pl.pallas_call(
        functools.partial(_gdn_prep_kernel, nbc=nbc, nb=nb),
        grid=(b, nb),
        in_specs=[pl.BlockSpec((1, GDN_HALO, w3), lambda i, j: (i, jnp.maximum(j * per - 1, 0), 0)),
                  pl.BlockSpec((1, BLK, w3), lambda i, j: (i, j, 0)),
                  pl.BlockSpec((1, GDN_HALO, w3), lambda i, j: (i, jnp.minimum((j + 1) * per, nb * per - 1), 0)),
                  pl.BlockSpec((1, BLK, 128), lambda i, j: (i, j, (ZG_W - 128) // 128)),
                  pl.BlockSpec((9, w3), lambda i, j: (0, 0)),
                  pl.BlockSpec((1, 128), lambda i, j: (0, 0)),
                  pl.BlockSpec((1, 128), lambda i, j: (0, 0))],
        out_specs=[pl.BlockSpec((1, BLK, w3), lambda i, j: (i, j, 0)),
                   pl.BlockSpec((1, BLK, 128), lambda i, j: (i, j, 0))],
        out_shape=[jax.ShapeDtypeStruct((b, l, w3), F32), jax.ShapeDtypeStruct((b, l, 128), F32)],
        scratch_shapes=[pltpu.VMEM((BLK + 2 * GDN_HALO, w3), F32)],
        compiler_params=_cparams(("arbitrary", "arbitrary")),
        name="gdn_prep",
    )(zg, zg, zg, zg, conv_w, a_neg, dt_bias)


def _heads(x):
    return jnp.concatenate([x[:, h * HEAD_DIM:(h + 1) * HEAD_DIM][None] for h in range(N_HEADS)], axis=0)


def _col_heads(x, lane0):
    return jnp.concatenate([jnp.broadcast_to(x[:, lane0 + h:lane0 + h + 1], (GDN_CHUNK, HEAD_DIM))[None]
                            for h in range(N_HEADS)], axis=0)


def _row_heads(xt, row0):
    return jnp.concatenate([jnp.broadcast_to(xt[row0 + h:row0 + h + 1, :], (HEAD_DIM, GDN_CHUNK))[None]
                            for h in range(N_HEADS)], axis=0)


def _bmm16(a, b):
    return _bmm(a.astype(BF16), b.astype(BF16))


def _gdn_gates(ga, rev):
    pos = lax.broadcasted_iota(jnp.int32, ga.shape, 0) & (GDN_CHUNK - 1)
    g2 = _chunk_cumsum(ga, pos, GDN_CHUNK, rev)
    return g2, g2.T, _chunk_total(ga, pos, GDN_CHUNK)


def _gdn_load(group):
    qkv_ref, ga_ref, (g2, g2t, tot2), cidx, d = group
    c = GDN_CHUNK
    rows = slice(cidx * c, (cidx + 1) * c)
    a0 = d * N_HEADS
    b0 = 2 * N_HEADS + d * N_HEADS
    return dict(gc=_col_heads(g2[rows], a0), gr=_row_heads(g2t[:, rows], a0), totc=_col_heads(tot2[rows], a0),
                beta=_col_heads(ga_ref[0, rows, :], b0), q=_heads(qkv_ref[0, rows, 0:D_GROUP]),
                k=_heads(qkv_ref[0, rows, D_GROUP:2 * D_GROUP]), v=_heads(qkv_ref[0, rows, 2 * D_GROUP:]))


def _gdn_par(groups, n_fwd, out):
    c = GDN_CHUNK
    ops = [_gdn_load(g) for g in groups]
    cat = lambda name: jnp.concatenate([o[name] for o in ops], axis=0)
    gc, gr, totc, beta, q3, k3, v3 = (cat(nm) for nm in ('gc', 'gr', 'totc', 'beta', 'q', 'k', 'v'))
    n_inst = q3.shape[0]
    shape = (n_inst, c, c)
    rev = lax.broadcasted_iota(jnp.int32, shape, 0) >= n_fwd * N_HEADS
    ri = lax.broadcasted_iota(jnp.int32, shape, 1)
    ci = lax.broadcasted_iota(jnp.int32, shape, 2)
    lag = jnp.where(rev, ci - ri, ri - ci)
    tri = lag >= 0
    strict = lag > 0
    same_sub = lax.shift_right_logical(ri, 4) == lax.shift_right_logical(ci, 4)
    eye = (ri == ci).astype(F32)

    lmat = jnp.where(tri, jnp.exp(jnp.minimum(gc - gr, 0.0)), 0.0)
    kb = k3 * beta
    k16 = k3.astype(BF16)
    kkt = _bmm_nt(kb.astype(BF16), k16)
    qkt = _bmm_nt(q3.astype(BF16), k16)
    yield
    amat = jnp.where(strict, kkt * lmat, 0.0)
    dmat = jnp.where(same_sub, amat, 0.0)
    lo = amat - dmat
    d2 = _bmm16(dmat, dmat)
    yield
    d4 = _bmm16(d2, d2)
    p1 = _bmm16(eye - dmat, eye + d2)
    yield
    d8 = _bmm16(d4, d4)
    yield
    p2 = _bmm16(eye + d4, eye + d8)
    yield
    tdiag = _bmm16(p1, p2)
    yield
    mmat = _bmm16(tdiag, lo)
    yield
    m2 = _bmm16(mmat, mmat)
    yield
    p3 = _bmm16(eye - mmat, eye + m2)
    yield
    tinv = _bmm16(p3, tdiag).astype(BF16)
    yield
    out.update(u=_bmm(tinv, (v3 * beta).astype(BF16)),
               w=_bmm(tinv, (kb * jnp.exp(gc)).astype(BF16)).astype(BF16),
               qk=jnp.where(tri, qkt * lmat, 0.0).astype(BF16), qd=(q3 * jnp.exp(gc)).astype(BF16),
               kd=(k3 * jnp.exp(totc - gc)).astype(BF16), cdec=jnp.exp(totc))


def _gdn_seq(p, st_box, n_steps, write):
    nh = N_HEADS
    for k in range(n_steps):
        sel = lambda a: jnp.concatenate([a[nh * k:nh * (k + 1)], a[nh * (n_steps + k):nh * (n_steps + k + 1)]], axis=0)
        st = st_box[0]
        s16 = st.astype(BF16)
        ws = _bmm(sel(p['w']), s16)
        qs = _bmm(sel(p['qd']), s16)
        yield
        v16 = (sel(p['u']) - ws).astype(BF16)
        o = qs + _bmm(sel(p['qk']), v16)
        st_box[0] = st * sel(p['cdec']) + jnp.einsum('gck,gcv->gkv', sel(p['kd']), v16, preferred_element_type=F32)
        write(k, jnp.concatenate([o[h] for h in range(nh)], axis=-1),
              jnp.concatenate([o[nh + h] for h in range(nh)], axis=-1))
        yield


def _interleave(*gens):
    gens = list(gens)
    while gens:
        for g in list(gens):
            try:
                next(g)
            except StopIteration:
                gens.remove(g)


def _gdn_kernel(xf_ref, gf_ref, xb_ref, gb_ref, of_ref, ob_ref, st_ref):
    @pl.when(pl.program_id(1) == 0)
    def _():
        st_ref[...] = jnp.zeros_like(st_ref)

    c = GDN_CHUNK
    nc = xf_ref.shape[1] // c
    gates_f = _gdn_gates(gf_ref[0], False)
    gates_b = _gdn_gates(gb_ref[0], True)
    fwd = lambda ci: (xf_ref, gf_ref, gates_f, ci, 0)
    bwd = lambda ci: (xb_ref, gb_ref, gates_b, ci, 1)

    def writer(first_f, first_b):
        def write(k, o_f, o_b):
            cf, cb = first_f + k, first_b - k
            of_ref[0, cf * c:(cf + 1) * c, :] = o_f
            ob_ref[0, cb * c:(cb + 1) * c, :] = o_b
        return write

    st_box = [jnp.concatenate([st_ref[0], st_ref[1]], axis=0)]
    g = GDN_BATCH
    pending = None
    for first in range(0, nc, g):
        p = {}
        par = _gdn_par([fwd(first + t) for t in range(g)] + [bwd(nc - 1 - first - t) for t in range(g)], g, p)
        _interleave(*([par] if pending is None else [par, pending]))
        pending = _gdn_seq(p, st_box, g, writer(first, nc - 1 - first))
    _interleave(pending)
    st_ref[0] = st_box[0][:N_HEADS]
    st_ref[1] = st_box[0][N_HEADS:]


def _gdn_scan(qkv, ga, nbc):
    b, l, w3 = qkv.shape
    nb = l // BLK
    xf, xb = _scan_specs(nbc, nb, (0,), (0,), w3)
    gf, gb = _scan_specs(nbc, nb, (0,), (0,), 128)
    outs = _scan_specs(nbc, nb, (0,), (0,))
    return pl.pallas_call(
        _gdn_kernel,
        grid=(b, nb),
        in_specs=xf + gf + xb + gb,
        out_specs=outs[0] + outs[1],
        out_shape=[jax.ShapeDtypeStruct((b, l, D_GROUP), F32)] * 2,
        scratch_shapes=[pltpu.VMEM((2, N_HEADS, HEAD_DIM, HEAD_DIM), F32)],
        compiler_params=_cparams(("arbitrary", "arbitrary")),
        name="gdn_scan",
    )(qkv, ga, qkv, ga)


def _cmul(ar, ai, hr, hi):
    return ar * hr - ai * hi, ar * hi + ai * hr


def _s5_pack(re, im):
    lead = re.shape[:-1]
    parts = jnp.stack([re.reshape(lead + (-1, S5_PART)), im.reshape(lead + (-1, S5_PART))], axis=-2)
    return parts.reshape(lead + (2 * S5_LANES,))


def _s5_kernel(u_ref, ws_ref, wc_ref, wt_ref, aux_ref, pw_ref, y_ref, h_ref, *, rev, parts):
    u = u_ref[0]
    rows, width = h_ref.shape
    pw = S5_PART
    pieces = [(slice(o, o + pw), slice(o + pw, o + 2 * pw)) for o in range(0, width, 2 * pw)]
    pos = lax.broadcasted_iota(jnp.int32, (rows // 8, 8, pw), 1)
    gw = pw // S5_STATE * S5_GROUP
    u_q = [jnp.concatenate([u[:, i * D_GROUP + q * gw:i * D_GROUP + (q + 1) * gw] for i in range(S5_CHUNK)], axis=1)
           for q in range(len(pieces))]
    for q, (re, im) in enumerate(pieces):
        x = _dot(u_q[q], ws_ref[q])
        xr = x[:, 0:pw].reshape(rows // 8, 8, pw)
        xi = x[:, pw:].reshape(rows // 8, 8, pw)
        for lvl, k in enumerate((1, 2, 4)):
            dr, di = _cmul(aux_ref[lvl:lvl + 1, re], aux_ref[lvl:lvl + 1, im],
                           pltpu.roll(xr, 8 - k if rev else k, 1), pltpu.roll(xi, 8 - k if rev else k, 1))
            valid = (pos < 8 - k) if rev else (pos >= k)
            xr = xr + jnp.where(valid, dr, 0.0)
            xi = xi + jnp.where(valid, di, 0.0)
        h_ref[:, re] = xr.reshape(rows, pw)
        h_ref[:, im] = xi.reshape(rows, pw)

    sub = lax.broadcasted_iota(jnp.int32, (8, pw), 0)
    edge = 0 if rev else 7

    def tile_step(t, carry):
        sl = pl.ds(pl.multiple_of(t * 8, 8), 8)
        out = []
        for (re, im), (cr, ci) in zip(pieces, carry):
            dr, di = _cmul(pw_ref[:, re], pw_ref[:, im], cr, ci)
            fr, fi = h_ref[sl, re] + dr, h_ref[sl, im] + di
            first = sub == (7 if rev else 0)
            h_ref[sl, re] = jnp.where(first, cr, pltpu.roll(fr, 7 if rev else 1, 0))
            h_ref[sl, im] = jnp.where(first, ci, pltpu.roll(fi, 7 if rev else 1, 0))
            out.append((jnp.broadcast_to(fr[edge:edge + 1, :], fr.shape),
                        jnp.broadcast_to(fi[edge:edge + 1, :], fi.shape)))
        return tuple(out)

    carry = tuple((jnp.zeros((8, pw), F32), jnp.zeros((8, pw), F32)) for _ in pieces)
    for row0, nrows in parts:
        t0, nt = row0 // 8, nrows // 8
        if rev:
            carry = lax.fori_loop(0, nt, lambda i, c, t0=t0, nt=nt: tile_step(t0 + nt - 1 - i, c), carry)
        else:
            carry = lax.fori_loop(0, nt, lambda i, c, t0=t0: tile_step(t0 + i, c), carry)
    y_q = [_dot(u_q[q], wt_ref[q]) + _dot(h_ref[:, re.start:im.stop].astype(BF16), wc_ref[q])
           for q, (re, im) in enumerate(pieces)]
    y_pos = [jnp.concatenate([y_q[q][:, i * gw:(i + 1) * gw] for q in range(len(pieces))], axis=1).astype(BF16)
             for i in range(S5_CHUNK)]
    perm = _chunk_select(BLK, True)
    n_chunks = BLK // S5_CHUNK
    for blk in range(rows // n_chunks):
        stacked = jnp.concatenate([y[blk * n_chunks:(blk + 1) * n_chunks] for y in y_pos], axis=0)
        y_ref[0, blk * BLK:(blk + 1) * BLK, :] = _dot(perm, stacked).astype(BF16)


def _s5_scan(u_c, weights, li, dd, parts):
    bsz, rows, w = u_c.shape
    pick = lambda a: pl.BlockSpec((None, None) + a.shape[2:], lambda b: (li, dd) + (0,) * (a.ndim - 2))
    n_tok = rows * S5_CHUNK
    return pl.pallas_call(
        functools.partial(_s5_kernel, rev=dd == 1, parts=parts),
        grid=(bsz,),
        in_specs=[pl.BlockSpec((1, rows, w), lambda b: (b, 0, 0))] + [pick(a) for a in weights],
        out_specs=pl.BlockSpec((1, n_tok, D_GROUP), lambda b: (b, 0, 0)),
        out_shape=jax.ShapeDtypeStruct((bsz, n_tok, D_GROUP), BF16),
        scratch_shapes=[pltpu.VMEM((rows, 2 * S5_LANES), F32)],
        compiler_params=_cparams(("arbitrary",)),
        name="s5_scan_bwd" if dd == 1 else "s5_scan_fwd",
    )(u_c, *weights)


def _s5_weights(lam_re, lam_im, log_dt, b_re, b_im, c_re, c_im):
    cch, ng = S5_CHUNK, S5_GROUPS
    nq = S5_LANES // S5_PART
    gl = ng // nq
    dt = jnp.exp(log_dt)[..., None]
    ang, dec = lam_im * dt, lam_re * dt

    def power(n_fwd, n_bwd=None):
        n = np.stack([n_fwd, n_fwd if n_bwd is None else n_bwd], axis=1).astype(np.float32)
        n = jnp.asarray(n).reshape(-1, 1, 2, 1, 1)
        mag = jnp.exp(n * dec)
        return mag * jnp.cos(n * ang), mag * jnp.sin(n * ang)

    pr, pi = power(np.arange(cch + 1))
    ar, ai = pr[1], pi[1]
    den = lam_re * lam_re + lam_im * lam_im
    nr, ni = ar - 1.0, ai
    fr = (nr * lam_re + ni * lam_im) / den
    fi = (ni * lam_re - nr * lam_im) / den
    bbr = fr[..., None] * b_re[:, None] - fi[..., None] * b_im[:, None]
    bbi = fr[..., None] * b_im[:, None] + fi[..., None] * b_re[:, None]
    eye = jnp.eye(gl, dtype=F32)
    pieces = lambda x, axis: x.reshape(x.shape[:axis] + (nq, gl) + x.shape[axis + 1:])
    idx = np.arange(cch)

    af_r, af_i = power(cch - 1 - idx, idx)
    ws = jnp.stack([af_r[..., None] * bbr - af_i[..., None] * bbi, af_r[..., None] * bbi + af_i[..., None] * bbr], -1)
    ws = jnp.einsum('ndrqgpcs,gh->drqngcshp', pieces(ws, 3), eye)
    ws = ws.reshape(ws.shape[:3] + (cch * gl * S5_GROUP, 2 * S5_PART))
    up_r, up_i = power(idx + 1, cch - idx)
    cr, ci = c_re[:, None], c_im[:, None]
    wc = jnp.stack([cr * up_r[..., None, :] - ci * up_i[..., None, :],
                    -(cr * up_i[..., None, :] + ci * up_r[..., None, :])], -1)
    wc = jnp.einsum('ndrqgcps,gh->drqshpngc', pieces(wc, 3), eye)
    wc = wc.reshape(wc.shape[:3] + (2 * S5_PART, cch * gl * S5_GROUP))
    ca_r = cr * pr[:cch, ..., None, :] - ci * pi[:cch, ..., None, :]
    ca_i = cr * pi[:cch, ..., None, :] + ci * pr[:cch, ..., None, :]
    tap = functools.partial(jnp.einsum, 'ndrgcp,drgpe->ndrgec', precision=lax.Precision.HIGHEST)
    taps = tap(ca_r, bbr) - tap(ca_i, bbi)
    lag = idx[None, :] - idx[:, None]
    tl = jnp.where((lag >= 0).reshape(cch, cch, 1, 1, 1, 1, 1), taps[np.maximum(lag, 0)], 0.0)
    is_bwd = (np.arange(2) == 1).reshape(1, 1, 1, 2, 1, 1, 1)
    wt = jnp.where(is_bwd, jnp.swapaxes(tl, 0, 1), tl)
    wt = jnp.einsum('jidrqgec,gh->drqjgeihc', pieces(wt, 4), eye)
    wt = wt.reshape(wt.shape[:3] + (cch * gl * S5_GROUP, cch * gl * S5_GROUP))

    def table(n_fwd, n_bwd):
        flat = lambda t: t.reshape(t.shape[:3] + (S5_LANES,))
        tr, ti = power(n_fwd, n_bwd)
        return jnp.moveaxis(_s5_pack(flat(tr), flat(ti)), 0, 2)

    doubling = cch * np.array([1, 2, 4, 0, 0, 0, 0, 0])
    aux = table(doubling, doubling)
    pw = table(cch * (np.arange(8) + 1), cch * (8 - np.arange(8)))
    return ws.astype(BF16), wc.astype(BF16), wt.astype(BF16), aux, pw


def _gated_norm(o, gate, gain, e):
    y = o * lax.rsqrt(_seg_sum(o * o, e) * (1.0 / HEAD_DIM) + EPS)
    if gain is not None:
        y = y * gain
    return y * _silu(gate)


def _gelu_tanh(x):
    return 0.5 * x * (1.0 + jnp.tanh(np.sqrt(2.0 / np.pi) * (x + 0.044715 * (x * x * x))))


def _out_mlp_kernel(c_ref, x_ref, mod_ref, haf_ref, hab_ref, hg_ref, rf_ref, rb_ref, rg_ref, gf_ref, gb_ref, gg_ref,
                    sf_ref, sb_ref, su_ref, hng_ref, gng_ref, sd_ref, glw_ref, glb_ref, wo_ref, n2_ref, w1_ref,
                    w2_ref, fg_ref, o_ref, *, final, nbc, off):
    m = mod_ref[0, 0]
    x0 = jnp.where(pl.program_id(1) + off < nbc, c_ref[0], x_ref[0])
    e = _head_eq(D_GROUP).astype(BF16)
    a = _gated_norm(haf_ref[0] + hab_ref[0], hg_ref[0], hng_ref[...], e)
    r = _gated_norm(rf_ref[0] + rb_ref[0], rg_ref[0], None, e)
    g = _gated_norm(gf_ref[0] + gb_ref[0], gg_ref[0], gng_ref[...], e)
    s = _gelu_tanh(sf_ref[0].astype(F32) + sb_ref[0].astype(F32) + su_ref[0] * sd_ref[...])
    s = s * jax.nn.sigmoid(_dot(s, glw_ref[...]) + glb_ref[...])
    y = jnp.zeros_like(x0)
    for i, part in enumerate((a, r, g, s)):
        y = y + _dot(part.astype(BF16), wo_ref[i * D_GROUP:(i + 1) * D_GROUP, :])
    x1 = x0 + m[2:3] * y
    h2 = _rms_mod(x1, n2_ref[...], m[3:4], m[4:5]).astype(BF16)
    acc = jnp.zeros_like(x1)
    for c0 in range(0, D_FF, FF_CHUNK):
        hid = jnp.maximum(_dot(h2, w1_ref[:, c0:c0 + FF_CHUNK]), 0.0)
        acc = acc + _dot((hid * hid).astype(BF16), w2_ref[c0:c0 + FF_CHUNK, :])
    x2 = x1 + m[5:6] * acc
    if final:
        x2 = x2 * lax.rsqrt(jnp.mean(x2 * x2, axis=-1, keepdims=True) + EPS) * fg_ref[...]
    o_ref[0] = x2


def _out_mlp(src, modv, za, zr, zg, zs, ha, ra, ga, sa, hng, gng, sd, glw, glb, wo, n2, w1, w2, fg, nbc, final):
    b, l, _ = za.shape
    d = src[0].shape[-1]
    nb = l // BLK
    off = nbc if final else 0
    rows = lambda width, c=0: pl.BlockSpec((1, BLK, width), lambda i, j: (i, j + off, c))
    const = lambda arr: pl.BlockSpec(arr.shape, lambda i, j: (0,) * arr.ndim)
    seg = lambda i, j: (i, jnp.where(j + off < nbc, 0, 1), 0, 0)
    weights = [hng, gng, sd, glw, glb, wo, n2, w1, w2, fg]
    args = [src[0], src[1], modv, ha[0], ha[1], za, ra[0], ra[1], zr, ga[0], ga[1], zg, sa[0], sa[1], zs] + weights
    in_specs = _seq_specs(src, nbc, off) + [pl.BlockSpec((1, 1, 8, d), seg),
                rows(D_GROUP), rows(D_GROUP), rows(D_GROUP, 2),
                rows(D_GROUP), rows(D_GROUP), rows(D_GROUP, 3),
                rows(D_GROUP), rows(D_GROUP), rows(D_GROUP, 3),
                rows(D_GROUP), rows(D_GROUP), rows(D_GROUP)] + [const(a) for a in weights]
    return pl.pallas_call(
        functools.partial(_out_mlp_kernel, final=final, nbc=nbc, off=off),
        grid=(b, nb - off),
        in_specs=in_specs,
        out_specs=pl.BlockSpec((1, BLK, d), lambda i, j: (i, j, 0)),
        out_shape=jax.ShapeDtypeStruct((b, l - off * BLK, d), F32),
        compiler_params=_cparams(("arbitrary", "arbitrary")),
        name="out_mlp_final" if final else "out_mlp",
    )(*args)


def kernel(x, c, ctx, c_ctx, mod_w, mod_b, norm1_g, norm2_g, w_in, hgrn_lb_logits, hgrn_norm_g, ret_decay_logit,
           gdn_conv_w, gdn_a_log, gdn_dt_bias, gdn_norm_g, s5_lam_re, s5_lam_im, s5_log_dt, s5_b_re, s5_b_im,
           s5_c_re, s5_c_im, s5_d, s5_glu_w, s5_glu_b, w_out, mlp_w1, mlp_w2, final_norm_g):
    bsz, t_lat, d = x.shape
    t_ctx = ctx.shape[1]
    depth = mod_w.shape[0]
    assert d == D_MODEL and t_ctx % BLK == 0 and t_lat % BLK == 0 and t_lat % GRID_W == 0
    assert t_ctx % (8 * S5_CHUNK) == 0 and t_lat % (8 * S5_CHUNK) == 0
    l = t_ctx + t_lat
    nbc = t_ctx // BLK

    src = (ctx.astype(F32), x.astype(F32), nbc)

    n_rows = -(-(bsz + 1) // 8) * 8
    cvecs = jnp.zeros((n_rows, d), F32).at[:bsz].set(c.astype(F32)).at[bsz].set(c_ctx.astype(F32))
    mod = _mod_proj(cvecs, mod_w.astype(F32), mod_b.astype(F32)).reshape(depth, n_rows, N_MOD, d)
    mod_lat = mod[:, :bsz]
    mod_ctx = jnp.broadcast_to(mod[:, bsz:bsz + 1], mod_lat.shape)
    modv = jnp.stack([mod_ctx, mod_lat], axis=2)
    modv = jnp.pad(modv, ((0, 0), (0, 0), (0, 0), (0, 8 - N_MOD), (0, 0)))

    sm = jax.nn.softmax(hgrn_lb_logits.astype(F32), axis=0)
    lbs = jnp.cumsum(sm, axis=0) - sm[:1]
    pos = jnp.arange(l, dtype=F32)
    half = HEAD_DIM // 2
    inv = ROPE_BASE ** (-jnp.arange(half, dtype=F32) / half)
    ang = pos[:, None] * inv[None, :]
    cos_t = jnp.tile(jnp.cos(ang), (1, 2 * N_HEADS))
    sin_t = jnp.tile(jnp.concatenate([-jnp.sin(ang), jnp.sin(ang)], axis=1), (1, N_HEADS))
    n_main = ZA_W + ZR_W + ZG_W - 128
    w_main = w_in.astype(BF16)
    w_ab = jnp.pad(w_main[..., n_main:n_main + 4 * N_HEADS], ((0, 0), (0, 0), (0, 128 - 4 * N_HEADS)))
    w_s = w_main[..., n_main + 4 * N_HEADS:]
    pad8 = lambda v: jnp.pad(v.astype(F32).reshape(1, 2 * N_HEADS), ((0, 0), (0, 128 - 2 * N_HEADS)))
    s5_w = _s5_weights(*(p.astype(F32) for p in (s5_lam_re, s5_lam_im, s5_log_dt, s5_b_re, s5_b_im, s5_c_re, s5_c_im)))
    s5_parts = ((0, t_ctx // S5_CHUNK), (t_ctx // S5_CHUNK, t_lat // S5_CHUNK))

    out = None
    for li in range(depth):
        final = li == depth - 1
        za, zr, zg, zs, u_c = _in_proj(src, l, modv[li], norm1_g[li].astype(F32).reshape(1, d),
                                       w_main[li], w_ab[li], w_s[li], nbc)
        ha = _hgrn_scan(za, lbs[li], nbc)
        lg = jax.nn.log_sigmoid(ret_decay_logit[li].astype(F32))
        ra = _ret_scan(zr, cos_t, sin_t, lg, nbc)
        qkv, gab = _gdn_prep(zg, gdn_conv_w[li].astype(F32).reshape(9, 3 * D_GROUP),
                             pad8(-jnp.exp(gdn_a_log[li].astype(F32))), pad8(gdn_dt_bias[li]), nbc)
        ga = _gdn_scan(qkv, gab, nbc)
        sa = [_s5_scan(u_c, s5_w, li, dd, s5_parts) for dd in range(2)]
        row = lambda v: v.astype(F32).reshape(1, -1)
        res = _out_mlp(src, modv[li], za, zr, zg, zs, ha, ra, ga, sa,
                       row(jnp.tile(hgrn_norm_g[li], N_HEADS)), row(jnp.tile(gdn_norm_g[li], N_HEADS)),
                       row(s5_d[li]), s5_glu_w[li].astype(F32), row(s5_glu_b[li]),
                       w_out[li].astype(BF16), row(norm2_g[li]), mlp_w1[li].astype(BF16), mlp_w2[li].astype(BF16),
                       row(final_norm_g), nbc, final)
        if final:
            out = res
        else:
            src = (res, res, 0)
    return out.astype(x.dtype)
```

```python
import functools

import jax
import jax.numpy as jnp
import numpy as np
from jax import lax
from jax.experimental import pallas as pl
from jax.experimental.pallas import tpu as pltpu

F32 = jnp.float32
BF16 = jnp.bfloat16

D_MODEL = 1024
D_GROUP = D_MODEL // 4
N_HEADS = 4
HEAD_DIM = D_GROUP // N_HEADS
D_FF = 4 * D_MODEL
N_MOD = 6
EPS = 1e-6
LB_FLOOR = 1e-30
LOG2E = 1.4426950408889634
GRID_W = 64
ROPE_BASE = 10000.0
S5_GROUP = 16
S5_GROUPS = D_GROUP // S5_GROUP
S5_STATE = 64
S5_LANES = S5_GROUPS * S5_STATE

BLK = 256
HG_CHUNK = 16
GDN_CHUNK = 64
GDN_SUB = 16
GDN_HALO = 128
GDN_BATCH = 4
S5_CHUNK = 4
S5_PART = 256
FF_CHUNK = 1024
V7X_VMEM_LIMIT = 56 * 1024 * 1024

ZA_W = 5 * D_GROUP
ZR_W = 4 * D_GROUP
ZG_W = 4 * D_GROUP + 128
ZS_W = D_GROUP


def _cparams(sem):
    return pltpu.CompilerParams(dimension_semantics=sem, vmem_limit_bytes=V7X_VMEM_LIMIT)


def _bwd_block(s, nbc, nb):
    return jnp.where(s < nbc, nbc - 1 - s, nb - 1 - (s - nbc))


def _dot(a, b):
    return jnp.dot(a, b, preferred_element_type=F32)


def _dot_nt(a, b):
    return lax.dot_general(a, b, (((1,), (1,)), ((), ())), preferred_element_type=F32)


def _dot_tn(a, b):
    return lax.dot_general(a, b, (((0,), (0,)), ((), ())), preferred_element_type=F32)


def _bmm(a, b):
    return jnp.einsum('gij,gjk->gik', a, b, preferred_element_type=F32)


def _bmm_nt(a, b):
    return jnp.einsum('gik,gjk->gij', a, b, preferred_element_type=F32)


def _silu(x):
    return x * jax.nn.sigmoid(x)


def _head_eq(n):
    r = lax.shift_right_logical(lax.broadcasted_iota(jnp.int32, (n, n), 0), 6)
    c = lax.shift_right_logical(lax.broadcasted_iota(jnp.int32, (n, n), 1), 6)
    return r == c


def _seg_sum(x, e):
    hi = x.astype(BF16)
    lo = (x - hi.astype(F32)).astype(BF16)
    return _dot(hi, e) + _dot(lo, e)


def _rms_mod(x, g, shift, scale):
    h = x * lax.rsqrt(jnp.mean(x * x, axis=-1, keepdims=True) + EPS) * g
    return h * (1.0 + scale) + shift


def _chunk_cumsum(x, pos, c, rev):
    n = x.shape[0]
    sh = 1
    while sh < c:
        if rev:
            x = x + jnp.where(pos < c - sh, pltpu.roll(x, n - sh, 0), 0.0)
        else:
            x = x + jnp.where(pos >= sh, pltpu.roll(x, sh, 0), 0.0)
        sh *= 2
    return x


def _chunk_total(x, pos, c):
    n = x.shape[0]
    sh = 1
    while sh < c:
        x = x + jnp.where(pos >= sh, pltpu.roll(x, sh, 0), pltpu.roll(x, n - (c - sh), 0))
        sh *= 2
    return x


def _mod_kernel(c_ref, w_ref, b_ref, o_ref):
    o_ref[0] = _dot(_silu(c_ref[...]), w_ref[0]) + b_ref[0]


def _mod_proj(cvecs, mod_w, mod_b):
    depth, d, n = mod_w.shape
    rows = cvecs.shape[0]
    tn = 1536
    return pl.pallas_call(
        _mod_kernel,
        grid=(depth, n // tn),
        in_specs=[pl.BlockSpec((rows, d), lambda l, j: (0, 0)),
                  pl.BlockSpec((1, d, tn), lambda l, j: (l, 0, j)),
                  pl.BlockSpec((1, 1, tn), lambda l, j: (l, 0, j))],
        out_specs=pl.BlockSpec((1, rows, tn), lambda l, j: (l, 0, j)),
        out_shape=jax.ShapeDtypeStruct((depth, rows, n), F32),
        compiler_params=_cparams(("arbitrary", "arbitrary")),
        name="mod_proj",
    )(cvecs, mod_w, mod_b.reshape(depth, 1, n))


def _chunk_select(n_tok, tokens_on_rows):
    a = lax.broadcasted_iota(jnp.int32, (n_tok, n_tok), 1 if tokens_on_rows else 0)
    t = lax.broadcasted_iota(jnp.int32, (n_tok, n_tok), 0 if tokens_on_rows else 1)
    n_chunks = n_tok // S5_CHUNK
    shift = n_chunks.bit_length() - 1
    assert n_chunks == 1 << shift
    return (t == (a & (n_chunks - 1)) * S5_CHUNK + lax.shift_right_logical(a, shift)).astype(BF16)


def _seq_specs(src, nbc, off):
    ctx_arr, lat_arr, shift = src
    d = ctx_arr.shape[-1]
    return [pl.BlockSpec((1, BLK, d), lambda i, j: (i, jnp.minimum(j + off, nbc - 1), 0)),
            pl.BlockSpec((1, BLK, d), lambda i, j: (i, jnp.maximum(j + off - shift, 0), 0))]


def _in_proj_kernel(c_ref, x_ref, mod_ref, g_ref, w_ref, wab_ref, ws_ref, za_ref, zr_ref, zg_ref, zs_ref, zc_ref, *,
                    nbc):
    m = mod_ref[0, 0]
    x = jnp.where(pl.program_id(1) < nbc, c_ref[0], x_ref[0])
    h = _rms_mod(x, g_ref[...], m[0:1], m[1:2]).astype(BF16)
    za_ref[0] = _dot(h, w_ref[:, 0:ZA_W])
    zr_ref[0] = _dot(h, w_ref[:, ZA_W:ZA_W + ZR_W])
    zg_ref[0, :, 0:ZG_W - 128] = _dot(h, w_ref[:, ZA_W + ZR_W:ZA_W + ZR_W + ZG_W - 128])
    zg_ref[0, :, ZG_W - 128:] = _dot(h, wab_ref[...])
    zs_ref[0] = _dot(h, ws_ref[...])
    zp = _dot(_chunk_select(BLK, False), zs_ref[0].astype(BF16)).astype(BF16)
    n_chunks = BLK // S5_CHUNK
    for pos in range(S5_CHUNK):
        zc_ref[0, :, pos * ZS_W:(pos + 1) * ZS_W] = zp[pos * n_chunks:(pos + 1) * n_chunks, :]


def _in_proj(src, l, modv, g, w, w_ab, w_s, nbc):
    b, _, d = src[0].shape
    nb = l // BLK
    row = lambda width: pl.BlockSpec((1, BLK, width), lambda i, j: (i, j, 0))
    const = lambda a: pl.BlockSpec(a.shape, lambda i, j: (0,) * a.ndim)
    return pl.pallas_call(
        functools.partial(_in_proj_kernel, nbc=nbc),
        grid=(b, nb),
        in_specs=_seq_specs(src, nbc, 0) + [
                  pl.BlockSpec((1, 1, 8, d), lambda i, j: (i, jnp.where(j < nbc, 0, 1), 0, 0)),
                  pl.BlockSpec((1, d), lambda i, j: (0, 0)), const(w), const(w_ab), const(w_s)],
        out_specs=[row(ZA_W), row(ZR_W), row(ZG_W), row(ZS_W),
                   pl.BlockSpec((1, BLK // S5_CHUNK, S5_CHUNK * ZS_W), lambda i, j: (i, j, 0))],
        out_shape=[jax.ShapeDtypeStruct((b, l, width), F32) for width in (ZA_W, ZR_W, ZG_W, ZS_W)]
                  + [jax.ShapeDtypeStruct((b, l // S5_CHUNK, S5_CHUNK * ZS_W), BF16)],
        compiler_params=_cparams(("arbitrary", "arbitrary")),
        name="in_proj",
    )(src[0], src[1], modv, g, w, w_ab, w_s)


def _hgrn_dir(q_raw, v, f_raw, lb, st, rev, out):
    n = q_raw.shape[0]
    c = HG_CHUNK
    pos = lax.broadcasted_iota(jnp.int32, (n, D_GROUP), 0) & (c - 1)
    heq = _head_eq(D_GROUP)
    e = heq.astype(BF16)
    qs = _silu(q_raw) * HEAD_DIM ** -0.5
    lbm = jnp.maximum(lb, LB_FLOOR)
    ex = jnp.exp(-jnp.abs(f_raw))
    inv = 1.0 / (1.0 + ex)
    sig_pos = jnp.where(f_raw >= 0, inv, ex * inv)
    sig_neg = jnp.where(f_raw >= 0, ex * inv, inv)
    logf = jnp.log(lbm * sig_neg + sig_pos) * LOG2E
    kk = (1.0 - lbm) * sig_neg
    bcum = _chunk_cumsum(logf, pos, c, rev)
    tot = _chunk_total(logf, pos, c)
    lk = jnp.log(kk) * LOG2E
    bk = bcum - lk

    nc, hc = n // c, c // 2
    split = lambda x: x.reshape(nc, 2, hc, D_GROUP)
    half = lambda x4, h: x4[:, h].reshape(nc * hc, D_GROUP)
    b4, bk4, lk4, v4, q4 = split(bcum), split(bk), split(lk), split(v), split(qs)
    bh = [half(b4, 0), half(b4, 1)]
    qh = [half(q4, 0), half(q4, 1)]
    pos8 = lax.broadcasted_iota(jnp.int32, (nc * hc, D_GROUP), 0) & (hc - 1)
    acc = [jnp.zeros((nc * hc, D_GROUP), F32), jnp.zeros((nc * hc, D_GROUP), F32)]
    qd = qs * jnp.exp2(bcum)
    kd = (kk * jnp.exp2(tot - bcum)).astype(BF16)
    vt = v.T.astype(BF16)
    cdec = jnp.exp2(tot)
    zeros = lambda r: jnp.zeros((r, D_GROUP), BF16)
    outs = [None] * nc
    for j in range(c):
        hj, jj = j // hc, j % hc
        row = lambda x4: jnp.broadcast_to(x4[:, hj, jj:jj + 1, :], (nc, hc, D_GROUP)).reshape(nc * hc, D_GROUP)
        bkj, lkj, vj = row(bk4), row(lk4), row(v4)
        other = 0 if rev else 1
        for ht in ((hj,) if hj == other else (hj, other)):
            term = qh[ht] * jnp.exp2(jnp.minimum(bh[ht] - bkj, lkj))
            contrib = _dot(term.astype(BF16), e) * vj
            if ht == hj:
                contrib = jnp.where((pos8 <= jj) if rev else (pos8 >= jj), contrib, 0.0)
            acc[ht] = acc[ht] + contrib

        ci = nc - 1 - j if rev else j
        rows = slice(ci * c, (ci + 1) * c)
        outs[ci] = _dot_nt(qd[rows], st)
        pieces = [zeros(ci * c), kd[rows], zeros(n - (ci + 1) * c)]
        kd_n = jnp.concatenate([p for p in pieces if p.shape[0]], axis=0)
        st = st * cdec[ci * c:ci * c + 1, :] + jnp.where(heq, _dot(vt, kd_n), 0.0)
        yield
    o = jnp.concatenate([acc[0].reshape(nc, 1, hc, D_GROUP), acc[1].reshape(nc, 1, hc, D_GROUP)],
                        axis=1).reshape(n, D_GROUP)
    out.append((o + jnp.concatenate(outs, axis=0), st))


def _hgrn_kernel(qf_ref, vf_ref, ff_ref, qb_ref, vb_ref, fb_ref, lb_ref, of_ref, ob_ref, st_ref):
    @pl.when(pl.program_id(1) == 0)
    def _():
        st_ref[...] = jnp.zeros_like(st_ref)

    res_f, res_b = [], []
    _interleave(_hgrn_dir(qf_ref[0], vf_ref[0], ff_ref[0], lb_ref[0:1, :], st_ref[0], False, res_f),
                _hgrn_dir(qb_ref[0], vb_ref[0], fb_ref[0], lb_ref[1:2, :], st_ref[1], True, res_b))
    of_ref[0], st_ref[0] = res_f[0]
    ob_ref[0], st_ref[1] = res_b[0]


def _scan_specs(nbc, nb, cols_f, cols_b, width=D_GROUP):
    fwd = [pl.BlockSpec((1, BLK, width), functools.partial(lambda i, s, c: (i, s, c), c=c)) for c in cols_f]
    bwd = [pl.BlockSpec((1, BLK, width), functools.partial(lambda i, s, c: (i, _bwd_block(s, nbc, nb), c), c=c))
           for c in cols_b]
    return fwd, bwd


def _hgrn_scan(za, lb, nbc):
    b, l, _ = za.shape
    nb = l // BLK
    fwd, bwd = _scan_specs(nbc, nb, (0, 1, 3), (0, 1, 4))
    outs = _scan_specs(nbc, nb, (0,), (0,))
    return pl.pallas_call(
        _hgrn_kernel,
        grid=(b, nb),
        in_specs=fwd + bwd + [pl.BlockSpec((2, D_GROUP), lambda i, s: (0, 0))],
        out_specs=outs[0] + outs[1],
        out_shape=[jax.ShapeDtypeStruct((b, l, D_GROUP), F32)] * 2,
        scratch_shapes=[pltpu.VMEM((2, D_GROUP, D_GROUP), F32)],
        compiler_params=_cparams(("arbitrary", "arbitrary")),
        name="hgrn_scan",
    )(za, za, za, za, za, za, lb)


def _rope(x, cos, sin_signed):
    lane = lax.broadcasted_iota(jnp.int32, x.shape, 1) & (HEAD_DIM - 1)
    half = HEAD_DIM // 2
    partner = jnp.where(lane < half, pltpu.roll(x, D_GROUP - half, 1), pltpu.roll(x, half, 1))
    return x * cos + partner * sin_signed


def _ret_dir(q, k, v, cos, sin, lg_ref, d, lg_lane, st, rev):
    n = q.shape[0]
    q = _rope(q, cos, sin)
    k = _rope(k, cos, sin) * HEAD_DIM ** -0.5
    ri = lax.broadcasted_iota(jnp.int32, (n, n), 0)
    ci = lax.broadcasted_iota(jnp.int32, (n, n), 1)
    rel = (ci - ri) if rev else (ri - ci)
    relf = jnp.maximum(rel, 0).astype(F32)
    t = lax.broadcasted_iota(jnp.int32, (n, D_GROUP), 0).astype(F32)
    lane_head = lax.shift_right_logical(lax.broadcasted_iota(jnp.int32, (n, D_GROUP), 1), 6)
    if rev:
        qdec, kdec = jnp.exp((n - t) * lg_lane), jnp.exp(t * lg_lane)
    else:
        qdec, kdec = jnp.exp((t + 1.0) * lg_lane), jnp.exp((n - 1.0 - t) * lg_lane)
    kb = k.astype(BF16)
    vb = v.astype(BF16)
    o = _dot((q * qdec).astype(BF16), st.astype(BF16))
    q_heads = jnp.concatenate([jnp.where(lane_head == h, q, 0.0).astype(BF16) for h in range(N_HEADS)], axis=0)
    dmats = jnp.concatenate([jnp.where(rel >= 0, jnp.exp(relf * lg_ref[d, h]), 0.0) for h in range(N_HEADS)], axis=0)
    o_heads = _dot((_dot_nt(q_heads, kb) * dmats).astype(BF16), vb)
    for h in range(N_HEADS):
        o = o + jnp.where(lane_head == h, o_heads[h * n:(h + 1) * n], 0.0)
    kv = _dot((k * kdec).T.astype(BF16), vb)
    st = st * jnp.exp(n * lg_lane) + jnp.where(_head_eq(D_GROUP), kv, 0.0)
    return o, st


def _ret_kernel(lg_ref, qf_ref, kf_ref, vf_ref, cf_ref, sf_ref, qb_ref, kb_ref, vb_ref, cb_ref, sb_ref,
                lgl_ref, of_ref, ob_ref, st_ref):
    @pl.when(pl.program_id(1) == 0)
    def _():
        st_ref[...] = jnp.zeros_like(st_ref)

    o, st = _ret_dir(qf_ref[0], kf_ref[0], vf_ref[0], cf_ref[...], sf_ref[...], lg_ref, 0,
                     lgl_ref[0:1, :], st_ref[0], False)
    of_ref[0] = o
    st_ref[0] = st
    o, st = _ret_dir(qb_ref[0], kb_ref[0], vb_ref[0], cb_ref[...], sb_ref[...], lg_ref, 1,
                     lgl_ref[1:2, :], st_ref[1], True)
    ob_ref[0] = o
    st_ref[1] = st


def _ret_scan(zr, cos_t, sin_t, lg, nbc):
    b, l, _ = zr.shape
    nb = l // BLK
    fwd, bwd = _scan_specs(nbc, nb, (0, 1, 2), (0, 1, 2))
    outs = _scan_specs(nbc, nb, (0,), (0,))
    tab_f = pl.BlockSpec((BLK, D_GROUP), lambda i, s: (s, 0))
    tab_b = pl.BlockSpec((BLK, D_GROUP), lambda i, s: (_bwd_block(s, nbc, nb), 0))
    lg_lane = jnp.repeat(lg, HEAD_DIM, axis=-1)
    return pl.pallas_call(
        _ret_kernel,
        grid=(b, nb),
        in_specs=[pl.BlockSpec(memory_space=pltpu.SMEM)] + fwd + [tab_f, tab_f] + bwd + [tab_b, tab_b]
                 + [pl.BlockSpec((2, D_GROUP), lambda i, s: (0, 0))],
        out_specs=outs[0] + outs[1],
        out_shape=[jax.ShapeDtypeStruct((b, l, D_GROUP), F32)] * 2,
        scratch_shapes=[pltpu.VMEM((2, D_GROUP, D_GROUP), F32)],
        compiler_params=_cparams(("arbitrary", "arbitrary")),
        name="ret_scan",
    )(lg, zr, zr, zr, cos_t, sin_t, zr, zr, zr, cos_t, sin_t, lg_lane)


def _gdn_prep_kernel(prev_ref, cur_ref, next_ref, ab_ref, cw_ref, an_ref, dtb_ref, qkv_ref, ga_ref, xs_ref,
                     *, nbc, nb):
    j = pl.program_id(1)
    is_ctx = j < nbc
    prev_ok = jnp.where(is_ctx, j > 0, j > nbc)
    next_ok = jnp.where(is_ctx, j < nbc - 1, j < nb - 1)
    n, halo = BLK, GDN_HALO
    xs_ref[0:halo, :] = jnp.where(prev_ok, prev_ref[0], 0.0)
    xs_ref[halo:halo + n, :] = cur_ref[0]
    xs_ref[halo + n:, :] = jnp.where(next_ok, next_ref[0], 0.0)
    colpos = lax.broadcasted_iota(jnp.int32, (n, 1), 0) & (GRID_W - 1)
    acc = jnp.zeros((n, 3 * D_GROUP), F32)
    for dr in (-1, 0, 1):
        for dc in (-1, 0, 1):
            win = xs_ref[pl.ds(halo + GRID_W * dr + dc, n), :]
            col_ok = (colpos >= 1) if dc == -1 else ((colpos <= GRID_W - 2) if dc == 1 else (colpos >= 0))
            ok = (is_ctx | col_ok) if dr == 0 else (jnp.logical_not(is_ctx) & col_ok)
            acc = acc + jnp.where(ok, win, 0.0) * cw_ref[(dr + 1) * 3 + (dc + 1):(dr + 1) * 3 + (dc + 2), :]
    xc = _silu(acc)
    e = _head_eq(D_GROUP).astype(BF16)
    q = xc[:, 0:D_GROUP]
    k = xc[:, D_GROUP:2 * D_GROUP]
    qkv_ref[0, :, 0:D_GROUP] = q * lax.rsqrt(_seg_sum(q * q, e) + EPS) * HEAD_DIM ** -0.5
    qkv_ref[0, :, D_GROUP:2 * D_GROUP] = k * lax.rsqrt(_seg_sum(k * k, e) + EPS)
    qkv_ref[0, :, 2 * D_GROUP:] = xc[:, 2 * D_GROUP:]
    ab = ab_ref[0]
    z = ab + dtb_ref[...]
    softplus = jnp.maximum(z, 0.0) + jnp.log(1.0 + jnp.exp(-jnp.abs(z)))
    lane = lax.broadcasted_iota(jnp.int32, ab.shape, 1)
    ga_ref[0] = jnp.where(lane < 2 * N_HEADS, an_ref[...] * softplus, jax.nn.sigmoid(ab))


def _gdn_prep(zg, conv_w, a_neg, dt_bias, nbc):
    b, l, _ = zg.shape
    nb = l // BLK
    w3 = 3 * D_GROUP
    per = BLK // GDN_HALO
    return pl.pallas_call(
        functools.partial(_gdn_prep_kernel, nbc=nbc, nb=nb),
        grid=(b, nb),
        in_specs=[pl.BlockSpec((1, GDN_HALO, w3), lambda i, j: (i, jnp.maximum(j * per - 1, 0), 0)),
                  pl.BlockSpec((1, BLK, w3), lambda i, j: (i, j, 0)),
                  pl.BlockSpec((1, GDN_HALO, w3), lambda i, j: (i, jnp.minimum((j + 1) * per, nb * per - 1), 0)),
                  pl.BlockSpec((1, BLK, 128), lambda i, j: (i, j, (ZG_W - 128) // 128)),
                  pl.BlockSpec((9, w3), lambda i, j: (0, 0)),
                  pl.BlockSpec((1, 128), lambda i, j: (0, 0)),
                  pl.BlockSpec((1, 128), lambda i, j: (0, 0))],
        out_specs=[pl.BlockSpec((1, BLK, w3), lambda i, j: (i, j, 0)),
                   pl.BlockSpec((1, BLK, 128), lambda i, j: (i, j, 0))],
        out_shape=[jax.ShapeDtypeStruct((b, l, w3), F32), jax.ShapeDtypeStruct((b, l, 128), F32)],
        scratch_shapes=[pltpu.VMEM((BLK + 2 * GDN_HALO, w3), F32)],
        compiler_params=_cparams(("arbitrary", "arbitrary")),
        name="gdn_prep",
    )(zg, zg, zg, zg, conv_w, a_neg, dt_bias)


def _heads(x):
    return jnp.concatenate([x[:, h * HEAD_DIM:(h + 1) * HEAD_DIM][None] for h in range(N_HEADS)], axis=0)


def _col_heads(x, lane0):
    return jnp.concatenate([jnp.broadcast_to(x[:, lane0 + h:lane0 + h + 1], (GDN_CHUNK, HEAD_DIM))[None]
                            for h in range(N_HEADS)], axis=0)


def _row_heads(xt, row0):
    return jnp.concatenate([jnp.broadcast_to(xt[row0 + h:row0 + h + 1, :], (HEAD_DIM, GDN_CHUNK))[None]
                            for h in range(N_HEADS)], axis=0)


def _bmm16(a, b):
    return _bmm(a.astype(BF16), b.astype(BF16))


def _gdn_gates(ga, rev):
    pos = lax.broadcasted_iota(jnp.int32, ga.shape, 0) & (GDN_CHUNK - 1)
    g2 = _chunk_cumsum(ga, pos, GDN_CHUNK, rev)
    return g2, g2.T, _chunk_total(ga, pos, GDN_CHUNK)


def _gdn_load(group):
    qkv_ref, ga_ref, (g2, g2t, tot2), cidx, d = group
    c = GDN_CHUNK
    rows = slice(cidx * c, (cidx + 1) * c)
    a0 = d * N_HEADS
    b0 = 2 * N_HEADS + d * N_HEADS
    return dict(gc=_col_heads(g2[rows], a0), gr=_row_heads(g2t[:, rows], a0), totc=_col_heads(tot2[rows], a0),
                beta=_col_heads(ga_ref[0, rows, :], b0), q=_heads(qkv_ref[0, rows, 0:D_GROUP]),
                k=_heads(qkv_ref[0, rows, D_GROUP:2 * D_GROUP]), v=_heads(qkv_ref[0, rows, 2 * D_GROUP:]))


def _gdn_par(groups, n_fwd, out):
    c = GDN_CHUNK
    ops = [_gdn_load(g) for g in groups]
    cat = lambda name: jnp.concatenate([o[name] for o in ops], axis=0)
    gc, gr, totc, beta, q3, k3, v3 = (cat(nm) for nm in ('gc', 'gr', 'totc', 'beta', 'q', 'k', 'v'))
    n_inst = q3.shape[0]
    shape = (n_inst, c, c)
    rev = lax.broadcasted_iota(jnp.int32, shape, 0) >= n_fwd * N_HEADS
    ri = lax.broadcasted_iota(jnp.int32, shape, 1)
    ci = lax.broadcasted_iota(jnp.int32, shape, 2)
    lag = jnp.where(rev, ci - ri, ri - ci)
    tri = lag >= 0
    strict = lag > 0
    same_sub = lax.shift_right_logical(ri, 4) == lax.shift_right_logical(ci, 4)
    eye = (ri == ci).astype(F32)

    lmat = jnp.where(tri, jnp.exp(jnp.minimum(gc - gr, 0.0)), 0.0)
    kb = k3 * beta
    k16 = k3.astype(BF16)
    kkt = _bmm_nt(kb.astype(BF16), k16)
    qkt = _bmm_nt(q3.astype(BF16), k16)
    yield
    amat = jnp.where(strict, kkt * lmat, 0.0)
    dmat = jnp.where(same_sub, amat, 0.0)
    lo = amat - dmat
    d2 = _bmm16(dmat, dmat)
    yield
    d4 = _bmm16(d2, d2)
    p1 = _bmm16(eye - dmat, eye + d2)
    yield
    d8 = _bmm16(d4, d4)
    yield
    p2 = _bmm16(eye + d4, eye + d8)
    yield
    tdiag = _bmm16(p1, p2)
    yield
    mmat = _bmm16(tdiag, lo)
    yield
    m2 = _bmm16(mmat, mmat)
    yield
    p3 = _bmm16(eye - mmat, eye + m2)
    yield
    tinv = _bmm16(p3, tdiag).astype(BF16)
    yield
    out.update(u=_bmm(tinv, (v3 * beta).astype(BF16)),
               w=_bmm(tinv, (kb * jnp.exp(gc)).astype(BF16)).astype(BF16),
               qk=jnp.where(tri, qkt * lmat, 0.0).astype(BF16), qd=(q3 * jnp.exp(gc)).astype(BF16),
               kd=(k3 * jnp.exp(totc - gc)).astype(BF16), cdec=jnp.exp(totc))


def _gdn_seq(p, st_box, n_steps, write):
    nh = N_HEADS
    for k in range(n_steps):
        sel = lambda a: jnp.concatenate([a[nh * k:nh * (k + 1)], a[nh * (n_steps + k):nh * (n_steps + k + 1)]], axis=0)
        st = st_box[0]
        s16 = st.astype(BF16)
        ws = _bmm(sel(p['w']), s16)
        qs = _bmm(sel(p['qd']), s16)
        yield
        v16 = (sel(p['u']) - ws).astype(BF16)
        o = qs + _bmm(sel(p['qk']), v16)
        st_box[0] = st * sel(p['cdec']) + jnp.einsum('gck,gcv->gkv', sel(p['kd']), v16, preferred_element_type=F32)
        write(k, jnp.concatenate([o[h] for h in range(nh)], axis=-1),
              jnp.concatenate([o[nh + h] for h in range(nh)], axis=-1))
        yield


def _interleave(*gens):
    gens = list(gens)
    while gens:
        for g in list(gens):
            try:
                next(g)
            except StopIteration:
                gens.remove(g)


def _gdn_kernel(xf_ref, gf_ref, xb_ref, gb_ref, of_ref, ob_ref, st_ref):
    @pl.when(pl.program_id(1) == 0)
    def _():
        st_ref[...] = jnp.zeros_like(st_ref)

    c = GDN_CHUNK
    nc = xf_ref.shape[1] // c
    gates_f = _gdn_gates(gf_ref[0], False)
    gates_b = _gdn_gates(gb_ref[0], True)
    fwd = lambda ci: (xf_ref, gf_ref, gates_f, ci, 0)
    bwd = lambda ci: (xb_ref, gb_ref, gates_b, ci, 1)

    def writer(first_f, first_b):
        def write(k, o_f, o_b):
            cf, cb = first_f + k, first_b - k
            of_ref[0, cf * c:(cf + 1) * c, :] = o_f
            ob_ref[0, cb * c:(cb + 1) * c, :] = o_b
        return write

    st_box = [jnp.concatenate([st_ref[0], st_ref[1]], axis=0)]
    g = GDN_BATCH
    pending = None
    for first in range(0, nc, g):
        p = {}
        par = _gdn_par([fwd(first + t) for t in range(g)] + [bwd(nc - 1 - first - t) for t in range(g)], g, p)
        _interleave(*([par] if pending is None else [par, pending]))
        pending = _gdn_seq(p, st_box, g, writer(first, nc - 1 - first))
    _interleave(pending)
    st_ref[0] = st_box[0][:N_HEADS]
    st_ref[1] = st_box[0][N_HEADS:]


def _gdn_scan(qkv, ga, nbc):
    b, l, w3 = qkv.shape
    nb = l // BLK
    xf, xb = _scan_specs(nbc, nb, (0,), (0,), w3)
    gf, gb = _scan_specs(nbc, nb, (0,), (0,), 128)
    outs = _scan_specs(nbc, nb, (0,), (0,))
    return pl.pallas_call(
        _gdn_kernel,
        grid=(b, nb),
        in_specs=xf + gf + xb + gb,
        out_specs=outs[0] + outs[1],
        out_shape=[jax.ShapeDtypeStruct((b, l, D_GROUP), F32)] * 2,
        scratch_shapes=[pltpu.VMEM((2, N_HEADS, HEAD_DIM, HEAD_DIM), F32)],
        compiler_params=_cparams(("arbitrary", "arbitrary")),
        name="gdn_scan",
    )(qkv, ga, qkv, ga)


def _cmul(ar, ai, hr, hi):
    return ar * hr - ai * hi, ar * hi + ai * hr


def _s5_pack(re, im):
    lead = re.shape[:-1]
    parts = jnp.stack([re.reshape(lead + (-1, S5_PART)), im.reshape(lead + (-1, S5_PART))], axis=-2)
    return parts.reshape(lead + (2 * S5_LANES,))


def _s5_kernel(u_ref, ws_ref, wc_ref, wt_ref, aux_ref, pw_ref, y_ref, h_ref, *, rev, parts):
    u = u_ref[0]
    rows, width = h_ref.shape
    pw = S5_PART
    pieces = [(slice(o, o + pw), slice(o + pw, o + 2 * pw)) for o in range(0, width, 2 * pw)]
    pos = lax.broadcasted_iota(jnp.int32, (rows // 8, 8, pw), 1)
    gw = pw // S5_STATE * S5_GROUP
    u_q = [jnp.concatenate([u[:, i * D_GROUP + q * gw:i * D_GROUP + (q + 1) * gw] for i in range(S5_CHUNK)], axis=1)
           for q in range(len(pieces))]
    for q, (re, im) in enumerate(pieces):
        x = _dot(u_q[q], ws_ref[q])
        xr = x[:, 0:pw].reshape(rows // 8, 8, pw)
        xi = x[:, pw:].reshape(rows // 8, 8, pw)
        for lvl, k in enumerate((1, 2, 4)):
            dr, di = _cmul(aux_ref[lvl:lvl + 1, re], aux_ref[lvl:lvl + 1, im],
                           pltpu.roll(xr, 8 - k if rev else k, 1), pltpu.roll(xi, 8 - k if rev else k, 1))
            valid = (pos < 8 - k) if rev else (pos >= k)
            xr = xr + jnp.where(valid, dr, 0.0)
            xi = xi + jnp.where(valid, di, 0.0)
        h_ref[:, re] = xr.reshape(rows, pw)
        h_ref[:, im] = xi.reshape(rows, pw)

    sub = lax.broadcasted_iota(jnp.int32, (8, pw), 0)
    edge = 0 if rev else 7

    def tile_step(t, carry):
        sl = pl.ds(pl.multiple_of(t * 8, 8), 8)
        out = []
        for (re, im), (cr, ci) in zip(pieces, carry):
            dr, di = _cmul(pw_ref[:, re], pw_ref[:, im], cr, ci)
            fr, fi = h_ref[sl, re] + dr, h_ref[sl, im] + di
            first = sub == (7 if rev else 0)
            h_ref[sl, re] = jnp.where(first, cr, pltpu.roll(fr, 7 if rev else 1, 0))
            h_ref[sl, im] = jnp.where(first, ci, pltpu.roll(fi, 7 if rev else 1, 0))
            out.append((jnp.broadcast_to(fr[edge:edge + 1, :], fr.shape),
                        jnp.broadcast_to(fi[edge:edge + 1, :], fi.shape)))
        return tuple(out)

    carry = tuple((jnp.zeros((8, pw), F32), jnp.zeros((8, pw), F32)) for _ in pieces)
    for row0, nrows in parts:
        t0, nt = row0 // 8, nrows // 8
        if rev:
            carry = lax.fori_loop(0, nt, lambda i, c, t0=t0, nt=nt: tile_step(t0 + nt - 1 - i, c), carry)
        else:
            carry = lax.fori_loop(0, nt, lambda i, c, t0=t0: tile_step(t0 + i, c), carry)
    y_q = [_dot(u_q[q], wt_ref[q]) + _dot(h_ref[:, re.start:im.stop].astype(BF16), wc_ref[q])
           for q, (re, im) in enumerate(pieces)]
    y_pos = [jnp.concatenate([y_q[q][:, i * gw:(i + 1) * gw] for q in range(len(pieces))], axis=1).astype(BF16)
             for i in range(S5_CHUNK)]
    perm = _chunk_select(BLK, True)
    n_chunks = BLK // S5_CHUNK
    for blk in range(rows // n_chunks):
        stacked = jnp.concatenate([y[blk * n_chunks:(blk + 1) * n_chunks] for y in y_pos], axis=0)
        y_ref[0, blk * BLK:(blk + 1) * BLK, :] = _dot(perm, stacked).astype(BF16)


def _s5_scan(u_c, weights, li, dd, parts):
    bsz, rows, w = u_c.shape
    pick = lambda a: pl.BlockSpec((None, None) + a.shape[2:], lambda b: (li, dd) + (0,) * (a.ndim - 2))
    n_tok = rows * S5_CHUNK
    return pl.pallas_call(
        functools.partial(_s5_kernel, rev=dd == 1, parts=parts),
        grid=(bsz,),
        in_specs=[pl.BlockSpec((1, rows, w), lambda b: (b, 0, 0))] + [pick(a) for a in weights],
        out_specs=pl.BlockSpec((1, n_tok, D_GROUP), lambda b: (b, 0, 0)),
        out_shape=jax.ShapeDtypeStruct((bsz, n_tok, D_GROUP), BF16),
        scratch_shapes=[pltpu.VMEM((rows, 2 * S5_LANES), F32)],
        compiler_params=_cparams(("arbitrary",)),
        name="s5_scan_bwd" if dd == 1 else "s5_scan_fwd",
    )(u_c, *weights)


def _s5_weights(lam_re, lam_im, log_dt, b_re, b_im, c_re, c_im):
    cch, ng = S5_CHUNK, S5_GROUPS
    nq = S5_LANES // S5_PART
    gl = ng // nq
    dt = jnp.exp(log_dt)[..., None]
    ang, dec = lam_im * dt, lam_re * dt

    def power(n_fwd, n_bwd=None):
        n = np.stack([n_fwd, n_fwd if n_bwd is None else n_bwd], axis=1).astype(np.float32)
        n = jnp.asarray(n).reshape(-1, 1, 2, 1, 1)
        mag = jnp.exp(n * dec)
        return mag * jnp.cos(n * ang), mag * jnp.sin(n * ang)

    pr, pi = power(np.arange(cch + 1))
    ar, ai = pr[1], pi[1]
    den = lam_re * lam_re + lam_im * lam_im
    nr, ni = ar - 1.0, ai
    fr = (nr * lam_re + ni * lam_im) / den
    fi = (ni * lam_re - nr * lam_im) / den
    bbr = fr[..., None] * b_re[:, None] - fi[..., None] * b_im[:, None]
    bbi = fr[..., None] * b_im[:, None] + fi[..., None] * b_re[:, None]
    eye = jnp.eye(gl, dtype=F32)
    pieces = lambda x, axis: x.reshape(x.shape[:axis] + (nq, gl) + x.shape[axis + 1:])
    idx = np.arange(cch)

    af_r, af_i = power(cch - 1 - idx, idx)
    ws = jnp.stack([af_r[..., None] * bbr - af_i[..., None] * bbi, af_r[..., None] * bbi + af_i[..., None] * bbr], -1)
    ws = jnp.einsum('ndrqgpcs,gh->drqngcshp', pieces(ws, 3), eye)
    ws = ws.reshape(ws.shape[:3] + (cch * gl * S5_GROUP, 2 * S5_PART))
    up_r, up_i = power(idx + 1, cch - idx)
    cr, ci = c_re[:, None], c_im[:, None]
    wc = jnp.stack([cr * up_r[..., None, :] - ci * up_i[..., None, :],
                    -(cr * up_i[..., None, :] + ci * up_r[..., None, :])], -1)
    wc = jnp.einsum('ndrqgcps,gh->drqshpngc', pieces(wc, 3), eye)
    wc = wc.reshape(wc.shape[:3] + (2 * S5_PART, cch * gl * S5_GROUP))
    ca_r = cr * pr[:cch, ..., None, :] - ci * pi[:cch, ..., None, :]
    ca_i = cr * pi[:cch, ..., None, :] + ci * pr[:cch, ..., None, :]
    tap = functools.partial(jnp.einsum, 'ndrgcp,drgpe->ndrgec', precision=lax.Precision.HIGHEST)
    taps = tap(ca_r, bbr) - tap(ca_i, bbi)
    lag = idx[None, :] - idx[:, None]
    tl = jnp.where((lag >= 0).reshape(cch, cch, 1, 1, 1, 1, 1), taps[np.maximum(lag, 0)], 0.0)
    is_bwd = (np.arange(2) == 1).reshape(1, 1, 1, 2, 1, 1, 1)
    wt = jnp.where(is_bwd, jnp.swapaxes(tl, 0, 1), tl)
    wt = jnp.einsum('jidrqgec,gh->drqjgeihc', pieces(wt, 4), eye)
    wt = wt.reshape(wt.shape[:3] + (cch * gl * S5_GROUP, cch * gl * S5_GROUP))

    def table(n_fwd, n_bwd):
        flat = lambda t: t.reshape(t.shape[:3] + (S5_LANES,))
        tr, ti = power(n_fwd, n_bwd)
        return jnp.moveaxis(_s5_pack(flat(tr), flat(ti)), 0, 2)

    doubling = cch * np.array([1, 2, 4, 0, 0, 0, 0, 0])
    aux = table(doubling, doubling)
    pw = table(cch * (np.arange(8) + 1), cch * (8 - np.arange(8)))
    return ws.astype(BF16), wc.astype(BF16), wt.astype(BF16), aux, pw


def _gated_norm(o, gate, gain, e):
    y = o * lax.rsqrt(_seg_sum(o * o, e) * (1.0 / HEAD_DIM) + EPS)
    if gain is not None:
        y = y * gain
    return y * _silu(gate)


def _gelu_tanh(x):
    return 0.5 * x * (1.0 + jnp.tanh(np.sqrt(2.0 / np.pi) * (x + 0.044715 * (x * x * x))))


def _out_mlp_kernel(c_ref, x_ref, mod_ref, haf_ref, hab_ref, hg_ref, rf_ref, rb_ref, rg_ref, gf_ref, gb_ref, gg_ref,
                    sf_ref, sb_ref, su_ref, hng_ref, gng_ref, sd_ref, glw_ref, glb_ref, wo_ref, n2_ref, w1_ref,
                    w2_ref, fg_ref, o_ref, *, final, nbc, off):
    m = mod_ref[0, 0]
    x0 = jnp.where(pl.program_id(1) + off < nbc, c_ref[0], x_ref[0])
    e = _head_eq(D_GROUP).astype(BF16)
    a = _gated_norm(haf_ref[0] + hab_ref[0], hg_ref[0], hng_ref[...], e)
    r = _gated_norm(rf_ref[0] + rb_ref[0], rg_ref[0], None, e)
    g = _gated_norm(gf_ref[0] + gb_ref[0], gg_ref[0], gng_ref[...], e)
    s = _gelu_tanh(sf_ref[0].astype(F32) + sb_ref[0].astype(F32) + su_ref[0] * sd_ref[...])
    s = s * jax.nn.sigmoid(_dot(s, glw_ref[...]) + glb_ref[...])
    y = jnp.zeros_like(x0)
    for i, part in enumerate((a, r, g, s)):
        y = y + _dot(part.astype(BF16), wo_ref[i * D_GROUP:(i + 1) * D_GROUP, :])
    x1 = x0 + m[2:3] * y
    h2 = _rms_mod(x1, n2_ref[...], m[3:4], m[4:5]).astype(BF16)
    acc = jnp.zeros_like(x1)
    for c0 in range(0, D_FF, FF_CHUNK):
        hid = jnp.maximum(_dot(h2, w1_ref[:, c0:c0 + FF_CHUNK]), 0.0)
        acc = acc + _dot((hid * hid).astype(BF16), w2_ref[c0:c0 + FF_CHUNK, :])
    x2 = x1 + m[5:6] * acc
    if final:
        x2 = x2 * lax.rsqrt(jnp.mean(x2 * x2, axis=-1, keepdims=True) + EPS) * fg_ref[...]
    o_ref[0] = x2


def _out_mlp(src, modv, za, zr, zg, zs, ha, ra, ga, sa, hng, gng, sd, glw, glb, wo, n2, w1, w2, fg, nbc, final):
    b, l, _ = za.shape
    d = src[0].shape[-1]
    nb = l // BLK
    off = nbc if final else 0
    rows = lambda width, c=0: pl.BlockSpec((1, BLK, width), lambda i, j: (i, j + off, c))
    const = lambda arr: pl.BlockSpec(arr.shape, lambda i, j: (0,) * arr.ndim)
    seg = lambda i, j: (i, jnp.where(j + off < nbc, 0, 1), 0, 0)
    weights = [hng, gng, sd, glw, glb, wo, n2, w1, w2, fg]
    args = [src[0], src[1], modv, ha[0], ha[1], za, ra[0], ra[1], zr, ga[0], ga[1], zg, sa[0], sa[1], zs] + weights
    in_specs = _seq_specs(src, nbc, off) + [pl.BlockSpec((1, 1, 8, d), seg),
                rows(D_GROUP), rows(D_GROUP), rows(D_GROUP, 2),
                rows(D_GROUP), rows(D_GROUP), rows(D_GROUP, 3),
                rows(D_GROUP), rows(D_GROUP), rows(D_GROUP, 3),
                rows(D_GROUP), rows(D_GROUP), rows(D_GROUP)] + [const(a) for a in weights]
    return pl.pallas_call(
        functools.partial(_out_mlp_kernel, final=final, nbc=nbc, off=off),
        grid=(b, nb - off),
        in_specs=in_specs,
        out_specs=pl.BlockSpec((1, BLK, d), lambda i, j: (i, j, 0)),
        out_shape=jax.ShapeDtypeStruct((b, l - off * BLK, d), F32),
        compiler_params=_cparams(("arbitrary", "arbitrary")),
        name="out_mlp_final" if final else "out_mlp",
    )(*args)


def kernel(x, c, ctx, c_ctx, mod_w, mod_b, norm1_g, norm2_g, w_in, hgrn_lb_logits, hgrn_norm_g, ret_decay_logit,
           gdn_conv_w, gdn_a_log, gdn_dt_bias, gdn_norm_g, s5_lam_re, s5_lam_im, s5_log_dt, s5_b_re, s5_b_im,
           s5_c_re, s5_c_im, s5_d, s5_glu_w, s5_glu_b, w_out, mlp_w1, mlp_w2, final_norm_g):
    bsz, t_lat, d = x.shape
    t_ctx = ctx.shape[1]
    depth = mod_w.shape[0]
    assert d == D_MODEL and t_ctx % BLK == 0 and t_lat % BLK == 0 and t_lat % GRID_W == 0
    assert t_ctx % (8 * S5_CHUNK) == 0 and t_lat % (8 * S5_CHUNK) == 0
    l = t_ctx + t_lat
    nbc = t_ctx // BLK

    src = (ctx.astype(F32), x.astype(F32), nbc)

    n_rows = -(-(bsz + 1) // 8) * 8
    cvecs = jnp.zeros((n_rows, d), F32).at[:bsz].set(c.astype(F32)).at[bsz].set(c_ctx.astype(F32))
    mod = _mod_proj(cvecs, mod_w.astype(F32), mod_b.astype(F32)).reshape(depth, n_rows, N_MOD, d)
    mod_lat = mod[:, :bsz]
    mod_ctx = jnp.broadcast_to(mod[:, bsz:bsz + 1], mod_lat.shape)
    modv = jnp.stack([mod_ctx, mod_lat], axis=2)
    modv = jnp.pad(modv, ((0, 0), (0, 0), (0, 0), (0, 8 - N_MOD), (0, 0)))

    sm = jax.nn.softmax(hgrn_lb_logits.astype(F32), axis=0)
    lbs = jnp.cumsum(sm, axis=0) - sm[:1]
    pos = jnp.arange(l, dtype=F32)
    half = HEAD_DIM // 2
    inv = ROPE_BASE ** (-jnp.arange(half, dtype=F32) / half)
    ang = pos[:, None] * inv[None, :]
    cos_t = jnp.tile(jnp.cos(ang), (1, 2 * N_HEADS))
    sin_t = jnp.tile(jnp.concatenate([-jnp.sin(ang), jnp.sin(ang)], axis=1), (1, N_HEADS))
    n_main = ZA_W + ZR_W + ZG_W - 128
    w_main = w_in.astype(BF16)
    w_ab = jnp.pad(w_main[..., n_main:n_main + 4 * N_HEADS], ((0, 0), (0, 0), (0, 128 - 4 * N_HEADS)))
    w_s = w_main[..., n_main + 4 * N_HEADS:]
    pad8 = lambda v: jnp.pad(v.astype(F32).reshape(1, 2 * N_HEADS), ((0, 0), (0, 128 - 2 * N_HEADS)))
    s5_w = _s5_weights(*(p.astype(F32) for p in (s5_lam_re, s5_lam_im, s5_log_dt, s5_b_re, s5_b_im, s5_c_re, s5_c_im)))
    s5_parts = ((0, t_ctx // S5_CHUNK), (t_ctx // S5_CHUNK, t_lat // S5_CHUNK))

    out = None
    for li in range(depth):
        final = li == depth - 1
        za, zr, zg, zs, u_c = _in_proj(src, l, modv[li], norm1_g[li].astype(F32).reshape(1, d),
                                       w_main[li], w_ab[li], w_s[li], nbc)
        ha = _hgrn_scan(za, lbs[li], nbc)
        lg = jax.nn.log_sigmoid(ret_decay_logit[li].astype(F32))
        ra = _ret_scan(zr, cos_t, sin_t, lg, nbc)
        qkv, gab = _gdn_prep(zg, gdn_conv_w[li].astype(F32).reshape(9, 3 * D_GROUP),
                             pad8(-jnp.exp(gdn_a_log[li].astype(F32))), pad8(gdn_dt_bias[li]), nbc)
        ga = _gdn_scan(qkv, gab, nbc)
        sa = [_s5_scan(u_c, s5_w, li, dd, s5_parts) for dd in range(2)]
        row = lambda v: v.astype(F32).reshape(1, -1)
        res = _out_mlp(src, modv[li], za, zr, zg, zs, ha, ra, ga, sa,
                       row(jnp.tile(hgrn_norm_g[li], N_HEADS)), row(jnp.tile(gdn_norm_g[li], N_HEADS)),
                       row(s5_d[li]), s5_glu_w[li].astype(F32), row(s5_glu_b[li]),
                       w_out[li].astype(BF16), row(norm2_g[li]), mlp_w1[li].astype(BF16), mlp_w2[li].astype(BF16),
                       row(final_norm_g), nbc, final)
        if final:
            out = res
        else:
            src = (res, res, 0)
    return out.astype(x.dtype)
```

```python
import functools

import jax
import jax.numpy as jnp
import numpy as np
from jax import lax
from jax.experimental import pallas as pl
from jax.experimental.pallas import tpu as pltpu

F32 = jnp.float32
BF16 = jnp.bfloat16

D_MODEL = 1024
D_GROUP = D_MODEL // 4
N_HEADS = 4
HEAD_DIM = D_GROUP // N_HEADS
D_FF = 4 * D_MODEL
N_MOD = 6
EPS = 1e-6
LB_FLOOR = 1e-30
LOG2E = 1.4426950408889634
GRID_W = 64
ROPE_BASE = 10000.0
S5_GROUP = 16
S5_GROUPS = D_GROUP // S5_GROUP
S5_STATE = 64
S5_LANES = S5_GROUPS * S5_STATE

BLK = 256
HG_CHUNK = 16
GDN_CHUNK = 64
GDN_SUB = 16
GDN_HALO = 128
GDN_BATCH = 4
S5_CHUNK = 4
S5_PART = 256
FF_CHUNK = 1024
V7X_VMEM_LIMIT = 56 * 1024 * 1024

ZA_W = 5 * D_GROUP
ZR_W = 4 * D_GROUP
ZG_W = 4 * D_GROUP + 128
ZS_W = D_GROUP


def _cparams(sem):
    return pltpu.CompilerParams(dimension_semantics=sem, vmem_limit_bytes=V7X_VMEM_LIMIT)


def _bwd_block(s, nbc, nb):
    return jnp.where(s < nbc, nbc - 1 - s, nb - 1 - (s - nbc))


def _dot(a, b):
    return jnp.dot(a, b, preferred_element_type=F32)


def _dot_nt(a, b):
    return lax.dot_general(a, b, (((1,), (1,)), ((), ())), preferred_element_type=F32)


def _dot_tn(a, b):
    return lax.dot_general(a, b, (((0,), (0,)), ((), ())), preferred_element_type=F32)


def _bmm(a, b):
    return jnp.einsum('gij,gjk->gik', a, b, preferred_element_type=F32)


def _bmm_nt(a, b):
    return jnp.einsum('gik,gjk->gij', a, b, preferred_element_type=F32)


def _silu(x):
    return x * jax.nn.sigmoid(x)


def _head_eq(n):
    r = lax.shift_right_logical(lax.broadcasted_iota(jnp.int32, (n, n), 0), 6)
    c = lax.shift_right_logical(lax.broadcasted_iota(jnp.int32, (n, n), 1), 6)
    return r == c


def _seg_sum(x, e):
    hi = x.astype(BF16)
    lo = (x - hi.astype(F32)).astype(BF16)
    return _dot(hi, e) + _dot(lo, e)


def _rms_mod(x, g, shift, scale):
    h = x * lax.rsqrt(jnp.mean(x * x, axis=-1, keepdims=True) + EPS) * g
    return h * (1.0 + scale) + shift


def _chunk_cumsum(x, pos, c, rev):
    n = x.shape[0]
    sh = 1
    while sh < c:
        if rev:
            x = x + jnp.where(pos < c - sh, pltpu.roll(x, n - sh, 0), 0.0)
        else:
            x = x + jnp.where(pos >= sh, pltpu.roll(x, sh, 0), 0.0)
        sh *= 2
    return x


def _chunk_total(x, pos, c):
    n = x.shape[0]
    sh = 1
    while sh < c:
        x = x + jnp.where(pos >= sh, pltpu.roll(x, sh, 0), pltpu.roll(x, n - (c - sh), 0))
        sh *= 2
    return x


def _mod_kernel(c_ref, w_ref, b_ref, o_ref):
    o_ref[0] = _dot(_silu(c_ref[...]), w_ref[0]) + b_ref[0]


def _mod_proj(cvecs, mod_w, mod_b):
    depth, d, n = mod_w.shape
    rows = cvecs.shape[0]
    tn = 1536
    return pl.pallas_call(
        _mod_kernel,
        grid=(depth, n // tn),
        in_specs=[pl.BlockSpec((rows, d), lambda l, j: (0, 0)),
                  pl.BlockSpec((1, d, tn), lambda l, j: (l, 0, j)),
                  pl.BlockSpec((1, 1, tn), lambda l, j: (l, 0, j))],
        out_specs=pl.BlockSpec((1, rows, tn), lambda l, j: (l, 0, j)),
        out_shape=jax.ShapeDtypeStruct((depth, rows, n), F32),
        compiler_params=_cparams(("arbitrary", "arbitrary")),
        name="mod_proj",
    )(cvecs, mod_w, mod_b.reshape(depth, 1, n))


def _chunk_select(n_tok, tokens_on_rows):
    a = lax.broadcasted_iota(jnp.int32, (n_tok, n_tok), 1 if tokens_on_rows else 0)
    t = lax.broadcasted_iota(jnp.int32, (n_tok, n_tok), 0 if tokens_on_rows else 1)
    n_chunks = n_tok // S5_CHUNK
    shift = n_chunks.bit_length() - 1
    assert n_chunks == 1 << shift
    return (t == (a & (n_chunks - 1)) * S5_CHUNK + lax.shift_right_logical(a, shift)).astype(BF16)


def _seq_specs(src, nbc, off):
    ctx_arr, lat_arr, shift = src
    d = ctx_arr.shape[-1]
    return [pl.BlockSpec((1, BLK, d), lambda i, j: (i, jnp.minimum(j + off, nbc - 1), 0)),
            pl.BlockSpec((1, BLK, d), lambda i, j: (i, jnp.maximum(j + off - shift, 0), 0))]


def _in_proj_kernel(c_ref, x_ref, mod_ref, g_ref, w_ref, wab_ref, ws_ref, za_ref, zr_ref, zg_ref, zs_ref, zc_ref, *,
                    nbc):
    m = mod_ref[0, 0]
    x = jnp.where(pl.program_id(1) < nbc, c_ref[0], x_ref[0])
    h = _rms_mod(x, g_ref[...], m[0:1], m[1:2]).astype(BF16)
    za_ref[0] = _dot(h, w_ref[:, 0:ZA_W])
    zr_ref[0] = _dot(h, w_ref[:, ZA_W:ZA_W + ZR_W])
    zg_ref[0, :, 0:ZG_W - 128] = _dot(h, w_ref[:, ZA_W + ZR_W:ZA_W + ZR_W + ZG_W - 128])
    zg_ref[0, :, ZG_W - 128:] = _dot(h, wab_ref[...])
    zs_ref[0] = _dot(h, ws_ref[...])
    zp = _dot(_chunk_select(BLK, False), zs_ref[0].astype(BF16)).astype(BF16)
    n_chunks = BLK // S5_CHUNK
    for pos in range(S5_CHUNK):
        zc_ref[0, :, pos * ZS_W:(pos + 1) * ZS_W] = zp[pos * n_chunks:(pos + 1) * n_chunks, :]


def _in_proj(src, l, modv, g, w, w_ab, w_s, nbc):
    b, _, d = src[0].shape
    nb = l // BLK
    row = lambda width: pl.BlockSpec((1, BLK, width), lambda i, j: (i, j, 0))
    const = lambda a: pl.BlockSpec(a.shape, lambda i, j: (0,) * a.ndim)
    return pl.pallas_call(
        functools.partial(_in_proj_kernel, nbc=nbc),
        grid=(b, nb),
        in_specs=_seq_specs(src, nbc, 0) + [
                  pl.BlockSpec((1, 1, 8, d), lambda i, j: (i, jnp.where(j < nbc, 0, 1), 0, 0)),
                  pl.BlockSpec((1, d), lambda i, j: (0, 0)), const(w), const(w_ab), const(w_s)],
        out_specs=[row(ZA_W), row(ZR_W), row(ZG_W), row(ZS_W),
                   pl.BlockSpec((1, BLK // S5_CHUNK, S5_CHUNK * ZS_W), lambda i, j: (i, j, 0))],
        out_shape=[jax.ShapeDtypeStruct((b, l, width), F32) for width in (ZA_W, ZR_W, ZG_W, ZS_W)]
                  + [jax.ShapeDtypeStruct((b, l // S5_CHUNK, S5_CHUNK * ZS_W), BF16)],
        compiler_params=_cparams(("arbitrary", "arbitrary")),
        name="in_proj",
    )(src[0], src[1], modv, g, w, w_ab, w_s)


def _hgrn_dir(q_raw, v, f_raw, lb, st, rev, out):
    n = q_raw.shape[0]
    c = HG_CHUNK
    pos = lax.broadcasted_iota(jnp.int32, (n, D_GROUP), 0) & (c - 1)
    heq = _head_eq(D_GROUP)
    e = heq.astype(BF16)
    qs = _silu(q_raw) * HEAD_DIM ** -0.5
    lbm = jnp.maximum(lb, LB_FLOOR)
    ex = jnp.exp(-jnp.abs(f_raw))
    inv = 1.0 / (1.0 + ex)
    sig_pos = jnp.where(f_raw >= 0, inv, ex * inv)
    sig_neg = jnp.where(f_raw >= 0, ex * inv, inv)
    logf = jnp.log(lbm * sig_neg + sig_pos) * LOG2E
    kk = (1.0 - lbm) * sig_neg
    bcum = _chunk_cumsum(logf, pos, c, rev)
    tot = _chunk_total(logf, pos, c)
    lk = jnp.log(kk) * LOG2E
    bk = bcum - lk

    nc, hc = n // c, c // 2
    split = lambda x: x.reshape(nc, 2, hc, D_GROUP)
    half = lambda x4, h: x4[:, h].reshape(nc * hc, D_GROUP)
    b4, bk4, lk4, v4, q4 = split(bcum), split(bk), split(lk), split(v), split(qs)
    bh = [half(b4, 0), half(b4, 1)]
    qh = [half(q4, 0), half(q4, 1)]
    pos8 = lax.broadcasted_iota(jnp.int32, (nc * hc, D_GROUP), 0) & (hc - 1)
    acc = [jnp.zeros((nc * hc, D_GROUP), F32), jnp.zeros((nc * hc, D_GROUP), F32)]
    qd = qs * jnp.exp2(bcum)
    kd = (kk * jnp.exp2(tot - bcum)).astype(BF16)
    vt = v.T.astype(BF16)
    cdec = jnp.exp2(tot)
    zeros = lambda r: jnp.zeros((r, D_GROUP), BF16)
    outs = [None] * nc
    for j in range(c):
        hj, jj = j // hc, j % hc
        row = lambda x4: jnp.broadcast_to(x4[:, hj, jj:jj + 1, :], (nc, hc, D_GROUP)).reshape(nc * hc, D_GROUP)
        bkj, lkj, vj = row(bk4), row(lk4), row(v4)
        other = 0 if rev else 1
        for ht in ((hj,) if hj == other else (hj, other)):
            term = qh[ht] * jnp.exp2(jnp.minimum(bh[ht] - bkj, lkj))
            contrib = _dot(term.astype(BF16), e) * vj
            if ht == hj:
                contrib = jnp.where((pos8 <= jj) if rev else (pos8 >= jj), contrib, 0.0)
            acc[ht] = acc[ht] + contrib

        ci = nc - 1 - j if rev else j
        rows = slice(ci * c, (ci + 1) * c)
        outs[ci] = _dot_nt(qd[rows], st)
        pieces = [zeros(ci * c), kd[rows], zeros(n - (ci + 1) * c)]
        kd_n = jnp.concatenate([p for p in pieces if p.shape[0]], axis=0)
        st = st * cdec[ci * c:ci * c + 1, :] + jnp.where(heq, _dot(vt, kd_n), 0.0)
        yield
    o = jnp.concatenate([acc[0].reshape(nc, 1, hc, D_GROUP), acc[1].reshape(nc, 1, hc, D_GROUP)],
                        axis=1).reshape(n, D_GROUP)
    out.append((o + jnp.concatenate(outs, axis=0), st))


def _hgrn_kernel(qf_ref, vf_ref, ff_ref, qb_ref, vb_ref, fb_ref, lb_ref, of_ref, ob_ref, st_ref):
    @pl.when(pl.program_id(1) == 0)
    def _():
        st_ref[...] = jnp.zeros_like(st_ref)

    res_f, res_b = [], []
    _interleave(_hgrn_dir(qf_ref[0], vf_ref[0], ff_ref[0], lb_ref[0:1, :], st_ref[0], False, res_f),
                _hgrn_dir(qb_ref[0], vb_ref[0], fb_ref[0], lb_ref[1:2, :], st_ref[1], True, res_b))
    of_ref[0], st_ref[0] = res_f[0]
    ob_ref[0], st_ref[1] = res_b[0]


def _scan_specs(nbc, nb, cols_f, cols_b, width=D_GROUP):
    fwd = [pl.BlockSpec((1, BLK, width), functools.partial(lambda i, s, c: (i, s, c), c=c)) for c in cols_f]
    bwd = [pl.BlockSpec((1, BLK, width), functools.partial(lambda i, s, c: (i, _bwd_block(s, nbc, nb), c), c=c))
           for c in cols_b]
    return fwd, bwd


def _hgrn_scan(za, lb, nbc):
    b, l, _ = za.shape
    nb = l // BLK
    fwd, bwd = _scan_specs(nbc, nb, (0, 1, 3), (0, 1, 4))
    outs = _scan_specs(nbc, nb, (0,), (0,))
    return pl.pallas_call(
        _hgrn_kernel,
        grid=(b, nb),
        in_specs=fwd + bwd + [pl.BlockSpec((2, D_GROUP), lambda i, s: (0, 0))],
        out_specs=outs[0] + outs[1],
        out_shape=[jax.ShapeDtypeStruct((b, l, D_GROUP), F32)] * 2,
        scratch_shapes=[pltpu.VMEM((2, D_GROUP, D_GROUP), F32)],
        compiler_params=_cparams(("arbitrary", "arbitrary")),
        name="hgrn_scan",
    )(za, za, za, za, za, za, lb)


def _rope(x, cos, sin_signed):
    lane = lax.broadcasted_iota(jnp.int32, x.shape, 1) & (HEAD_DIM - 1)
    half = HEAD_DIM // 2
    partner = jnp.where(lane < half, pltpu.roll(x, D_GROUP - half, 1), pltpu.roll(x, half, 1))
    return x * cos + partner * sin_signed


def _ret_dir(q, k, v, cos, sin, lg_ref, d, lg_lane, st, rev):
    n = q.shape[0]
    q = _rope(q, cos, sin)
    k = _rope(k, cos, sin) * HEAD_DIM ** -0.5
    ri = lax.broadcasted_iota(jnp.int32, (n, n), 0)
    ci = lax.broadcasted_iota(jnp.int32, (n, n), 1)
    rel = (ci - ri) if rev else (ri - ci)
    relf = jnp.maximum(rel, 0).astype(F32)
    t = lax.broadcasted_iota(jnp.int32, (n, D_GROUP), 0).astype(F32)
    lane_head = lax.shift_right_logical(lax.broadcasted_iota(jnp.int32, (n, D_GROUP), 1), 6)
    if rev:
        qdec, kdec = jnp.exp((n - t) * lg_lane), jnp.exp(t * lg_lane)
    else:
        qdec, kdec = jnp.exp((t + 1.0) * lg_lane), jnp.exp((n - 1.0 - t) * lg_lane)
    kb = k.astype(BF16)
    vb = v.astype(BF16)
    o = _dot((q * qdec).astype(BF16), st.astype(BF16))
    q_heads = jnp.concatenate([jnp.where(lane_head == h, q, 0.0).astype(BF16) for h in range(N_HEADS)], axis=0)
    dmats = jnp.concatenate([jnp.where(rel >= 0, jnp.exp(relf * lg_ref[d, h]), 0.0) for h in range(N_HEADS)], axis=0)
    o_heads = _dot((_dot_nt(q_heads, kb) * dmats).astype(BF16), vb)
    for h in range(N_HEADS):
        o = o + jnp.where(lane_head == h, o_heads[h * n:(h + 1) * n], 0.0)
    kv = _dot((k * kdec).T.astype(BF16), vb)
    st = st * jnp.exp(n * lg_lane) + jnp.where(_head_eq(D_GROUP), kv, 0.0)
    return o, st


def _ret_kernel(lg_ref, qf_ref, kf_ref, vf_ref, cf_ref, sf_ref, qb_ref, kb_ref, vb_ref, cb_ref, sb_ref,
                lgl_ref, of_ref, ob_ref, st_ref):
    @pl.when(pl.program_id(1) == 0)
    def _():
        st_ref[...] = jnp.zeros_like(st_ref)

    o, st = _ret_dir(qf_ref[0], kf_ref[0], vf_ref[0], cf_ref[...], sf_ref[...], lg_ref, 0,
                     lgl_ref[0:1, :], st_ref[0], False)
    of_ref[0] = o
    st_ref[0] = st
    o, st = _ret_dir(qb_ref[0], kb_ref[0], vb_ref[0], cb_ref[...], sb_ref[...], lg_ref, 1,
                     lgl_ref[1:2, :], st_ref[1], True)
    ob_ref[0] = o
    st_ref[1] = st


def _ret_scan(zr, cos_t, sin_t, lg, nbc):
    b, l, _ = zr.shape
    nb = l // BLK
    fwd, bwd = _scan_specs(nbc, nb, (0, 1, 2), (0, 1, 2))
    outs = _scan_specs(nbc, nb, (0,), (0,))
    tab_f = pl.BlockSpec((BLK, D_GROUP), lambda i, s: (s, 0))
    tab_b = pl.BlockSpec((BLK, D_GROUP), lambda i, s: (_bwd_block(s, nbc, nb), 0))
    lg_lane = jnp.repeat(lg, HEAD_DIM, axis=-1)
    return pl.pallas_call(
        _ret_kernel,
        grid=(b, nb),
        in_specs=[pl.BlockSpec(memory_space=pltpu.SMEM)] + fwd + [tab_f, tab_f] + bwd + [tab_b, tab_b]
                 + [pl.BlockSpec((2, D_GROUP), lambda i, s: (0, 0))],
        out_specs=outs[0] + outs[1],
        out_shape=[jax.ShapeDtypeStruct((b, l, D_GROUP), F32)] * 2,
        scratch_shapes=[pltpu.VMEM((2, D_GROUP, D_GROUP), F32)],
        compiler_params=_cparams(("arbitrary", "arbitrary")),
        name="ret_scan",
    )(lg, zr, zr, zr, cos_t, sin_t, zr, zr, zr, cos_t, sin_t, lg_lane)


def _gdn_prep_kernel(prev_ref, cur_ref, next_ref, ab_ref, cw_ref, an_ref, dtb_ref, qkv_ref, ga_ref, xs_ref,
                     *, nbc, nb):
    j = pl.program_id(1)
    is_ctx = j < nbc
    prev_ok = jnp.where(is_ctx, j > 0, j > nbc)
    next_ok = jnp.where(is_ctx, j < nbc - 1, j < nb - 1)
    n, halo = BLK, GDN_HALO
    xs_ref[0:halo, :] = jnp.where(prev_ok, prev_ref[0], 0.0)
    xs_ref[halo:halo + n, :] = cur_ref[0]
    xs_ref[halo + n:, :] = jnp.where(next_ok, next_ref[0], 0.0)
    colpos = lax.broadcasted_iota(jnp.int32, (n, 1), 0) & (GRID_W - 1)
    acc = jnp.zeros((n, 3 * D_GROUP), F32)
    for dr in (-1, 0, 1):
        for dc in (-1, 0, 1):
            win = xs_ref[pl.ds(halo + GRID_W * dr + dc, n), :]
            col_ok = (colpos >= 1) if dc == -1 else ((colpos <= GRID_W - 2) if dc == 1 else (colpos >= 0))
            ok = (is_ctx | col_ok) if dr == 0 else (jnp.logical_not(is_ctx) & col_ok)
            acc = acc + jnp.where(ok, win, 0.0) * cw_ref[(dr + 1) * 3 + (dc + 1):(dr + 1) * 3 + (dc + 2), :]
    xc = _silu(acc)
    e = _head_eq(D_GROUP).astype(BF16)
    q = xc[:, 0:D_GROUP]
    k = xc[:, D_GROUP:2 * D_GROUP]
    qkv_ref[0, :, 0:D_GROUP] = q * lax.rsqrt(_seg_sum(q * q, e) + EPS) * HEAD_DIM ** -0.5
    qkv_ref[0, :, D_GROUP:2 * D_GROUP] = k * lax.rsqrt(_seg_sum(k * k, e) + EPS)
    qkv_ref[0, :, 2 * D_GROUP:] = xc[:, 2 * D_GROUP:]
    ab = ab_ref[0]
    z = ab + dtb_ref[...]
    softplus = jnp.maximum(z, 0.0) + jnp.log(1.0 + jnp.exp(-jnp.abs(z)))
    lane = lax.broadcasted_iota(jnp.int32, ab.shape, 1)
    ga_ref[0] = jnp.where(lane < 2 * N_HEADS, an_ref[...] * softplus, jax.nn.sigmoid(ab))


def _gdn_prep(zg, conv_w, a_neg, dt_bias, nbc):
    b, l, _ = zg.shape
    nb = l // BLK
    w3 = 3 * D_GROUP
    per = BLK // GDN_HALO
    return pl.pallas_call(
        functools.partial(_gdn_prep_kernel, nbc=nbc, nb=nb),
        grid=(b, nb),
        in_specs=[pl.BlockSpec((1, GDN_HALO, w3), lambda i, j: (i, jnp.maximum(j * per - 1, 0), 0)),
                  pl.BlockSpec((1, BLK, w3), lambda i, j: (i, j, 0)),
                  pl.BlockSpec((1, GDN_HALO, w3), lambda i, j: (i, jnp.minimum((j + 1) * per, nb * per - 1), 0)),
                  pl.BlockSpec((1, BLK, 128), lambda i, j: (i, j, (ZG_W - 128) // 128)),
                  pl.BlockSpec((9, w3), lambda i, j: (0, 0)),
                  pl.BlockSpec((1, 128), lambda i, j: (0, 0)),
                  pl.BlockSpec((1, 128), lambda i, j: (0, 0))],
        out_specs=[pl.BlockSpec((1, BLK, w3), lambda i, j: (i, j, 0)),
                   pl.BlockSpec((1, BLK, 128), lambda i, j: (i, j, 0))],
        out_shape=[jax.ShapeDtypeStruct((b, l, w3), F32), jax.ShapeDtypeStruct((b, l, 128), F32)],
        scratch_shapes=[pltpu.VMEM((BLK + 2 * GDN_HALO, w3), F32)],
        compiler_params=_cparams(("arbitrary", "arbitrary")),
        name="gdn_prep",
    )(zg, zg, zg, zg, conv_w, a_neg, dt_bias)


def _heads(x):
    return jnp.concatenate([x[:, h * HEAD_DIM:(h + 1) * HEAD_DIM][None] for h in range(N_HEADS)], axis=0)


def _col_heads(x, lane0):
    return jnp.concatenate([jnp.broadcast_to(x[:, lane0 + h:lane0 + h + 1], (GDN_CHUNK, HEAD_DIM))[None]
                            for h in range(N_HEADS)], axis=0)


def _row_heads(xt, row0):
    return jnp.concatenate([jnp.broadcast_to(xt[row0 + h:row0 + h + 1, :], (HEAD_DIM, GDN_CHUNK))[None]
                            for h in range(N_HEADS)], axis=0)


def _bmm16(a, b):
    return _bmm(a.astype(BF16), b.astype(BF16))


def _gdn_gates(ga, rev):
    pos = lax.broadcasted_iota(jnp.int32, ga.shape, 0) & (GDN_CHUNK - 1)
    g2 = _chunk_cumsum(ga, pos, GDN_CHUNK, rev)
    return g2, g2.T, _chunk_total(ga, pos, GDN_CHUNK)


def _gdn_load(group):
    qkv_ref, ga_ref, (g2, g2t, tot2), cidx, d = group
    c = GDN_CHUNK
    rows = slice(cidx * c, (cidx + 1) * c)
    a0 = d * N_HEADS
    b0 = 2 * N_HEADS + d * N_HEADS
    return dict(gc=_col_heads(g2[rows], a0), gr=_row_heads(g2t[:, rows], a0), totc=_col_heads(tot2[rows], a0),
                beta=_col_heads(ga_ref[0, rows, :], b0), q=_heads(qkv_ref[0, rows, 0:D_GROUP]),
                k=_heads(qkv_ref[0, rows, D_GROUP:2 * D_GROUP]), v=_heads(qkv_ref[0, rows, 2 * D_GROUP:]))


def _gdn_par(groups, n_fwd, out):
    c = GDN_CHUNK
    ops = [_gdn_load(g) for g in groups]
    cat = lambda name: jnp.concatenate([o[name] for o in ops], axis=0)
    gc, gr, totc, beta, q3, k3, v3 = (cat(nm) for nm in ('gc', 'gr', 'totc', 'beta', 'q', 'k', 'v'))
    n_inst = q3.shape[0]
    shape = (n_inst, c, c)
    rev = lax.broadcasted_iota(jnp.int32, shape, 0) >= n_fwd * N_HEADS
    ri = lax.broadcasted_iota(jnp.int32, shape, 1)
    ci = lax.broadcasted_iota(jnp.int32, shape, 2)
    lag = jnp.where(rev, ci - ri, ri - ci)
    tri = lag >= 0
    strict = lag > 0
    same_sub = lax.shift_right_logical(ri, 4) == lax.shift_right_logical(ci, 4)
    eye = (ri == ci).astype(F32)

    lmat = jnp.where(tri, jnp.exp(jnp.minimum(gc - gr, 0.0)), 0.0)
    kb = k3 * beta
    k16 = k3.astype(BF16)
    kkt = _bmm_nt(kb.astype(BF16), k16)
    qkt = _bmm_nt(q3.astype(BF16), k16)
    yield
    amat = jnp.where(strict, kkt * lmat, 0.0)
    dmat = jnp.where(same_sub, amat, 0.0)
    lo = amat - dmat
    d2 = _bmm16(dmat, dmat)
    yield
    d4 = _bmm16(d2, d2)
    p1 = _bmm16(eye - dmat, eye + d2)
    yield
    d8 = _bmm16(d4, d4)
    yield
    p2 = _bmm16(eye + d4, eye + d8)
    yield
    tdiag = _bmm16(p1, p2)
    yield
    mmat = _bmm16(tdiag, lo)
    yield
    m2 = _bmm16(mmat, mmat)
    yield
    p3 = _bmm16(eye - mmat, eye + m2)
    yield
    tinv = _bmm16(p3, tdiag).astype(BF16)
    yield
    out.update(u=_bmm(tinv, (v3 * beta).astype(BF16)),
               w=_bmm(tinv, (kb * jnp.exp(gc)).astype(BF16)).astype(BF16),
               qk=jnp.where(tri, qkt * lmat, 0.0).astype(BF16), qd=(q3 * jnp.exp(gc)).astype(BF16),
               kd=(k3 * jnp.exp(totc - gc)).astype(BF16), cdec=jnp.exp(totc))


def _gdn_seq(p, st_box, n_steps, write):
    nh = N_HEADS
    for k in range(n_steps):
        sel = lambda a: jnp.concatenate([a[nh * k:nh * (k + 1)], a[nh * (n_steps + k):nh * (n_steps + k + 1)]], axis=0)
        st = st_box[0]
        s16 = st.astype(BF16)
        ws = _bmm(sel(p['w']), s16)
        qs = _bmm(sel(p['qd']), s16)
        yield
        v16 = (sel(p['u']) - ws).astype(BF16)
        o = qs + _bmm(sel(p['qk']), v16)
        st_box[0] = st * sel(p['cdec']) + jnp.einsum('gck,gcv->gkv', sel(p['kd']), v16, preferred_element_type=F32)
        write(k, jnp.concatenate([o[h] for h in range(nh)], axis=-1),
              jnp.concatenate([o[nh + h] for h in range(nh)], axis=-1))
        yield


def _interleave(*gens):
    gens = list(gens)
    while gens:
        for g in list(gens):
            try:
                next(g)
            except StopIteration:
                gens.remove(g)


def _gdn_kernel(xf_ref, gf_ref, xb_ref, gb_ref, of_ref, ob_ref, st_ref):
    @pl.when(pl.program_id(1) == 0)
    def _():
        st_ref[...] = jnp.zeros_like(st_ref)

    c = GDN_CHUNK
    nc = xf_ref.shape[1] // c
    gates_f = _gdn_gates(gf_ref[0], False)
    gates_b = _gdn_gates(gb_ref[0], True)
    fwd = lambda ci: (xf_ref, gf_ref, gates_f, ci, 0)
    bwd = lambda ci: (xb_ref, gb_ref, gates_b, ci, 1)

    def writer(first_f, first_b):
        def write(k, o_f, o_b):
            cf, cb = first_f + k, first_b - k
            of_ref[0, cf * c:(cf + 1) * c, :] = o_f
            ob_ref[0, cb * c:(cb + 1) * c, :] = o_b
        return write

    st_box = [jnp.concatenate([st_ref[0], st_ref[1]], axis=0)]
    g = GDN_BATCH
    pending = None
    for first in range(0, nc, g):
        p = {}
        par = _gdn_par([fwd(first + t) for t in range(g)] + [bwd(nc - 1 - first - t) for t in range(g)], g, p)
        _interleave(*([par] if pending is None else [par, pending]))
        pending = _gdn_seq(p, st_box, g, writer(first, nc - 1 - first))
    _interleave(pending)
    st_ref[0] = st_box[0][:N_HEADS]
    st_ref[1] = st_box[0][N_HEADS:]


def _gdn_scan(qkv, ga, nbc):
    b, l, w3 = qkv.shape
    nb = l // BLK
    xf, xb = _scan_specs(nbc, nb, (0,), (0,), w3)
    gf, gb = _scan_specs(nbc, nb, (0,), (0,), 128)
    outs = _scan_specs(nbc, nb, (0,), (0,))
    return pl.pallas_call(
        _gdn_kernel,
        grid=(b, nb),
        in_specs=xf + gf + xb + gb,
        out_specs=outs[0] + outs[1],
        out_shape=[jax.ShapeDtypeStruct((b, l, D_GROUP), F32)] * 2,
        scratch_shapes=[pltpu.VMEM((2, N_HEADS, HEAD_DIM, HEAD_DIM), F32)],
        compiler_params=_cparams(("arbitrary", "arbitrary")),
        name="gdn_scan",
    )(qkv, ga, qkv, ga)


def _cmul(ar, ai, hr, hi):
    return ar * hr - ai * hi, ar * hi + ai * hr


def _s5_pack(re, im):
    lead = re.shape[:-1]
    parts = jnp.stack([re.reshape(lead + (-1, S5_PART)), im.reshape(lead + (-1, S5_PART))], axis=-2)
    return parts.reshape(lead + (2 * S5_LANES,))


def _s5_kernel(u_ref, ws_ref, wc_ref, wt_ref, aux_ref, pw_ref, y_ref, h_ref, *, rev, parts):
    u = u_ref[0]
    rows, width = h_ref.shape
    pw = S5_PART
    pieces = [(slice(o, o + pw), slice(o + pw, o + 2 * pw)) for o in range(0, width, 2 * pw)]
    pos = lax.broadcasted_iota(jnp.int32, (rows // 8, 8, pw), 1)
    gw = pw // S5_STATE * S5_GROUP
    u_q = [jnp.concatenate([u[:, i * D_GROUP + q * gw:i * D_GROUP + (q + 1) * gw] for i in range(S5_CHUNK)], axis=1)
           for q in range(len(pieces))]
    for q, (re, im) in enumerate(pieces):
        x = _dot(u_q[q], ws_ref[q])
        xr = x[:, 0:pw].reshape(rows // 8, 8, pw)
        xi = x[:, pw:].reshape(rows // 8, 8, pw)
        for lvl, k in enumerate((1, 2, 4)):
            dr, di = _cmul(aux_ref[lvl:lvl + 1, re], aux_ref[lvl:lvl + 1, im],
                           pltpu.roll(xr, 8 - k if rev else k, 1), pltpu.roll(xi, 8 - k if rev else k, 1))
            valid = (pos < 8 - k) if rev else (pos >= k)
            xr = xr + jnp.where(valid, dr, 0.0)
            xi = xi + jnp.where(valid, di, 0.0)
        h_ref[:, re] = xr.reshape(rows, pw)
        h_ref[:, im] = xi.reshape(rows, pw)

    sub = lax.broadcasted_iota(jnp.int32, (8, pw), 0)
    edge = 0 if rev else 7

    def tile_step(t, carry):
        sl = pl.ds(pl.multiple_of(t * 8, 8), 8)
        out = []
        for (re, im), (cr, ci) in zip(pieces, carry):
            dr, di = _cmul(pw_ref[:, re], pw_ref[:, im], cr, ci)
            fr, fi = h_ref[sl, re] + dr, h_ref[sl, im] + di
            first = sub == (7 if rev else 0)
            h_ref[sl, re] = jnp.where(first, cr, pltpu.roll(fr, 7 if rev else 1, 0))
            h_ref[sl, im] = jnp.where(first, ci, pltpu.roll(fi, 7 if rev else 1, 0))
            out.append((jnp.broadcast_to(fr[edge:edge + 1, :], fr.shape),
                        jnp.broadcast_to(fi[edge:edge + 1, :], fi.shape)))
        return tuple(out)

    carry = tuple((jnp.zeros((8, pw), F32), jnp.zeros((8, pw), F32)) for _ in pieces)
    for row0, nrows in parts:
        t0, nt = row0 // 8, nrows // 8
        if rev:
            carry = lax.fori_loop(0, nt, lambda i, c, t0=t0, nt=nt: tile_step(t0 + nt - 1 - i, c), carry)
        else:
            carry = lax.fori_loop(0, nt, lambda i, c, t0=t0: tile_step(t0 + i, c), carry)
    y_q = [_dot(u_q[q], wt_ref[q]) + _dot(h_ref[:, re.start:im.stop].astype(BF16), wc_ref[q])
           for q, (re, im) in enumerate(pieces)]
    y_pos = [jnp.concatenate([y_q[q][:, i * gw:(i + 1) * gw] for q in range(len(pieces))], axis=1).astype(BF16)
             for i in range(S5_CHUNK)]
    perm = _chunk_select(BLK, True)
    n_chunks = BLK // S5_CHUNK
    for blk in range(rows // n_chunks):
        stacked = jnp.concatenate([y[blk * n_chunks:(blk + 1) * n_chunks] for y in y_pos], axis=0)
        y_ref[0, blk * BLK:(blk + 1) * BLK, :] = _dot(perm, stacked).astype(BF16)


def _s5_scan(u_c, weights, li, dd, parts):
    bsz, rows, w = u_c.shape
    pick = lambda a: pl.BlockSpec((None, None) + a.shape[2:], lambda b: (li, dd) + (0,) * (a.ndim - 2))
    n_tok = rows * S5_CHUNK
    return pl.pallas_call(
        functools.partial(_s5_kernel, rev=dd == 1, parts=parts),
        grid=(bsz,),
        in_specs=[pl.BlockSpec((1, rows, w), lambda b: (b, 0, 0))] + [pick(a) for a in weights],
        out_specs=pl.BlockSpec((1, n_tok, D_GROUP), lambda b: (b, 0, 0)),
        out_shape=jax.ShapeDtypeStruct((bsz, n_tok, D_GROUP), BF16),
        scratch_shapes=[pltpu.VMEM((rows, 2 * S5_LANES), F32)],
        compiler_params=_cparams(("arbitrary",)),
        name="s5_scan_bwd" if dd == 1 else "s5_scan_fwd",
    )(u_c, *weights)


def _s5_weights(lam_re, lam_im, log_dt, b_re, b_im, c_re, c_im):
    cch, ng = S5_CHUNK, S5_GROUPS
    nq = S5_LANES // S5_PART
    gl = ng // nq
    dt = jnp.exp(log_dt)[..., None]
    ang, dec = lam_im * dt, lam_re * dt

    def power(n_fwd, n_bwd=None):
        n = np.stack([n_fwd, n_fwd if n_bwd is None else n_bwd], axis=1).astype(np.float32)
        n = jnp.asarray(n).reshape(-1, 1, 2, 1, 1)
        mag = jnp.exp(n * dec)
        return mag * jnp.cos(n * ang), mag * jnp.sin(n * ang)

    pr, pi = power(np.arange(cch + 1))
    ar, ai = pr[1], pi[1]
    den = lam_re * lam_re + lam_im * lam_im
    nr, ni = ar - 1.0, ai
    fr = (nr * lam_re + ni * lam_im) / den
    fi = (ni * lam_re - nr * lam_im) / den
    bbr = fr[..., None] * b_re[:, None] - fi[..., None] * b_im[:, None]
    bbi = fr[..., None] * b_im[:, None] + fi[..., None] * b_re[:, None]
    eye = jnp.eye(gl, dtype=F32)
    pieces = lambda x, axis: x.reshape(x.shape[:axis] + (nq, gl) + x.shape[axis + 1:])
    idx = np.arange(cch)

    af_r, af_i = power(cch - 1 - idx, idx)
    ws = jnp.stack([af_r[..., None] * bbr - af_i[..., None] * bbi, af_r[..., None] * bbi + af_i[..., None] * bbr], -1)
    ws = jnp.einsum('ndrqgpcs,gh->drqngcshp', pieces(ws, 3), eye)
    ws = ws.reshape(ws.shape[:3] + (cch * gl * S5_GROUP, 2 * S5_PART))
    up_r, up_i = power(idx + 1, cch - idx)
    cr, ci = c_re[:, None], c_im[:, None]
    wc = jnp.stack([cr * up_r[..., None, :] - ci * up_i[..., None, :],
                    -(cr * up_i[..., None, :] + ci * up_r[..., None, :])], -1)
    wc = jnp.einsum('ndrqgcps,gh->drqshpngc', pieces(wc, 3), eye)
    wc = wc.reshape(wc.shape[:3] + (2 * S5_PART, cch * gl * S5_GROUP))
    ca_r = cr * pr[:cch, ..., None, :] - ci * pi[:cch, ..., None, :]
    ca_i = cr * pi[:cch, ..., None, :] + ci * pr[:cch, ..., None, :]
    tap = functools.partial(jnp.einsum, 'ndrgcp,drgpe->ndrgec', precision=lax.Precision.HIGHEST)
    taps = tap(ca_r, bbr) - tap(ca_i, bbi)
    lag = idx[None, :] - idx[:, None]
    tl = jnp.where((lag >= 0).reshape(cch, cch, 1, 1, 1, 1, 1), taps[np.maximum(lag, 0)], 0.0)
    is_bwd = (np.arange(2) == 1).reshape(1, 1, 1, 2, 1, 1, 1)
    wt = jnp.where(is_bwd, jnp.swapaxes(tl, 0, 1), tl)
    wt = jnp.einsum('jidrqgec,gh->drqjgeihc', pieces(wt, 4), eye)
    wt = wt.reshape(wt.shape[:3] + (cch * gl * S5_GROUP, cch * gl * S5_GROUP))

    def table(n_fwd, n_bwd):
        flat = lambda t: t.reshape(t.shape[:3] + (S5_LANES,))
        tr, ti = power(n_fwd, n_bwd)
        return jnp.moveaxis(_s5_pack(flat(tr), flat(ti)), 0, 2)

    doubling = cch * np.array([1, 2, 4, 0, 0, 0, 0, 0])
    aux = table(doubling, doubling)
    pw = table(cch * (np.arange(8) + 1), cch * (8 - np.arange(8)))
    return ws.astype(BF16), wc.astype(BF16), wt.astype(BF16), aux, pw


def _gated_norm(o, gate, gain, e):
    y = o * lax.rsqrt(_seg_sum(o * o, e) * (1.0 / HEAD_DIM) + EPS)
    if gain is not None:
        y = y * gain
    return y * _silu(gate)


def _gelu_tanh(x):
    return 0.5 * x * (1.0 + jnp.tanh(np.sqrt(2.0 / np.pi) * (x + 0.044715 * (x * x * x))))


def _out_mlp_kernel(c_ref, x_ref, mod_ref, haf_ref, hab_ref, hg_ref, rf_ref, rb_ref, rg_ref, gf_ref, gb_ref, gg_ref,
                    sf_ref, sb_ref, su_ref, hng_ref, gng_ref, sd_ref, glw_ref, glb_ref, wo_ref, n2_ref, w1_ref,
                    w2_ref, fg_ref, o_ref, *, final, nbc, off):
    m = mod_ref[0, 0]
    x0 = jnp.where(pl.program_id(1) + off < nbc, c_ref[0], x_ref[0])
    e = _head_eq(D_GROUP).astype(BF16)
    a = _gated_norm(haf_ref[0] + hab_ref[0], hg_ref[0], hng_ref[...], e)
    r = _gated_norm(rf_ref[0] + rb_ref[0], rg_ref[0], None, e)
    g = _gated_norm(gf_ref[0] + gb_ref[0], gg_ref[0], gng_ref[...], e)
    s = _gelu_tanh(sf_ref[0].astype(F32) + sb_ref[0].astype(F32) + su_ref[0] * sd_ref[...])
    s = s * jax.nn.sigmoid(_dot(s.astype(BF16), glw_ref[...]) + glb_ref[...])
    y = jnp.zeros_like(x0)
    for i, part in enumerate((a, r, g, s)):
        y = y + _dot(part.astype(BF16), wo_ref[i * D_GROUP:(i + 1) * D_GROUP, :])
    x1 = x0 + m[2:3] * y
    h2 = _rms_mod(x1, n2_ref[...], m[3:4], m[4:5]).astype(BF16)
    acc = jnp.zeros_like(x1)
    for c0 in range(0, D_FF, FF_CHUNK):
        hid = jnp.maximum(_dot(h2, w1_ref[:, c0:c0 + FF_CHUNK]), 0.0)
        acc = acc + _dot((hid * hid).astype(BF16), w2_ref[c0:c0 + FF_CHUNK, :])
    x2 = x1 + m[5:6] * acc
    if final:
        x2 = x2 * lax.rsqrt(jnp.mean(x2 * x2, axis=-1, keepdims=True) + EPS) * fg_ref[...]
    o_ref[0] = x2


def _out_mlp(src, modv, za, zr, zg, zs, ha, ra, ga, sa, hng, gng, sd, glw, glb, wo, n2, w1, w2, fg, nbc, final):
    b, l, _ = za.shape
    d = src[0].shape[-1]
    nb = l // BLK
    off = nbc if final else 0
    rows = lambda width, c=0: pl.BlockSpec((1, BLK, width), lambda i, j: (i, j + off, c))
    const = lambda arr: pl.BlockSpec(arr.shape, lambda i, j: (0,) * arr.ndim)
    seg = lambda i, j: (i, jnp.where(j + off < nbc, 0, 1), 0, 0)
    weights = [hng, gng, sd, glw, glb, wo, n2, w1, w2, fg]
    args = [src[0], src[1], modv, ha[0], ha[1], za, ra[0], ra[1], zr, ga[0], ga[1], zg, sa[0], sa[1], zs] + weights
    in_specs = _seq_specs(src, nbc, off) + [pl.BlockSpec((1, 1, 8, d), seg),
                rows(D_GROUP), rows(D_GROUP), rows(D_GROUP, 2),
                rows(D_GROUP), rows(D_GROUP), rows(D_GROUP, 3),
                rows(D_GROUP), rows(D_GROUP), rows(D_GROUP, 3),
                rows(D_GROUP), rows(D_GROUP), rows(D_GROUP)] + [const(a) for a in weights]
    return pl.pallas_call(
        functools.partial(_out_mlp_kernel, final=final, nbc=nbc, off=off),
        grid=(b, nb - off),
        in_specs=in_specs,
        out_specs=pl.BlockSpec((1, BLK, d), lambda i, j: (i, j, 0)),
        out_shape=jax.ShapeDtypeStruct((b, l - off * BLK, d), F32),
        compiler_params=_cparams(("arbitrary", "arbitrary")),
        name="out_mlp_final" if final else "out_mlp",
    )(*args)


def kernel(x, c, ctx, c_ctx, mod_w, mod_b, norm1_g, norm2_g, w_in, hgrn_lb_logits, hgrn_norm_g, ret_decay_logit,
           gdn_conv_w, gdn_a_log, gdn_dt_bias, gdn_norm_g, s5_lam_re, s5_lam_im, s5_log_dt, s5_b_re, s5_b_im,
           s5_c_re, s5_c_im, s5_d, s5_glu_w, s5_glu_b, w_out, mlp_w1, mlp_w2, final_norm_g):
    bsz, t_lat, d = x.shape
    t_ctx = ctx.shape[1]
    depth = mod_w.shape[0]
    assert d == D_MODEL and t_ctx % BLK == 0 and t_lat % BLK == 0 and t_lat % GRID_W == 0
    assert t_ctx % (8 * S5_CHUNK) == 0 and t_lat % (8 * S5_CHUNK) == 0
    l = t_ctx + t_lat
    nbc = t_ctx // BLK

    src = (ctx.astype(F32), x.astype(F32), nbc)

    n_rows = -(-(bsz + 1) // 8) * 8
    cvecs = jnp.zeros((n_rows, d), F32).at[:bsz].set(c.astype(F32)).at[bsz].set(c_ctx.astype(F32))
    mod = _mod_proj(cvecs, mod_w.astype(F32), mod_b.astype(F32)).reshape(depth, n_rows, N_MOD, d)
    mod_lat = mod[:, :bsz]
    mod_ctx = jnp.broadcast_to(mod[:, bsz:bsz + 1], mod_lat.shape)
    modv = jnp.stack([mod_ctx, mod_lat], axis=2)
    modv = jnp.pad(modv, ((0, 0), (0, 0), (0, 0), (0, 8 - N_MOD), (0, 0)))

    sm = jax.nn.softmax(hgrn_lb_logits.astype(F32), axis=0)
    lbs = jnp.cumsum(sm, axis=0) - sm[:1]
    pos = jnp.arange(l, dtype=F32)
    half = HEAD_DIM // 2
    inv = ROPE_BASE ** (-jnp.arange(half, dtype=F32) / half)
    ang = pos[:, None] * inv[None, :]
    cos_t = jnp.tile(jnp.cos(ang), (1, 2 * N_HEADS))
    sin_t = jnp.tile(jnp.concatenate([-jnp.sin(ang), jnp.sin(ang)], axis=1), (1, N_HEADS))
    n_main = ZA_W + ZR_W + ZG_W - 128
    w_main = w_in.astype(BF16)
    w_ab = jnp.pad(w_main[..., n_main:n_main + 4 * N_HEADS], ((0, 0), (0, 0), (0, 128 - 4 * N_HEADS)))
    w_s = w_main[..., n_main + 4 * N_HEADS:]
    pad8 = lambda v: jnp.pad(v.astype(F32).reshape(1, 2 * N_HEADS), ((0, 0), (0, 128 - 2 * N_HEADS)))
    s5_w = _s5_weights(*(p.astype(F32) for p in (s5_lam_re, s5_lam_im, s5_log_dt, s5_b_re, s5_b_im, s5_c_re, s5_c_im)))
    s5_parts = ((0, t_ctx // S5_CHUNK), (t_ctx // S5_CHUNK, t_lat // S5_CHUNK))

    out = None
    for li in range(depth):
        final = li == depth - 1
        za, zr, zg, zs, u_c = _in_proj(src, l, modv[li], norm1_g[li].astype(F32).reshape(1, d),
                                       w_main[li], w_ab[li], w_s[li], nbc)
        ha = _hgrn_scan(za, lbs[li], nbc)
        lg = jax.nn.log_sigmoid(ret_decay_logit[li].astype(F32))
        ra = _ret_scan(zr, cos_t, sin_t, lg, nbc)
        qkv, gab = _gdn_prep(zg, gdn_conv_w[li].astype(F32).reshape(9, 3 * D_GROUP),
                             pad8(-jnp.exp(gdn_a_log[li].astype(F32))), pad8(gdn_dt_bias[li]), nbc)
        ga = _gdn_scan(qkv, gab, nbc)
        sa = [_s5_scan(u_c, s5_w, li, dd, s5_parts) for dd in range(2)]
        row = lambda v: v.astype(F32).reshape(1, -1)
        res = _out_mlp(src, modv[li], za, zr, zg, zs, ha, ra, ga, sa,
                       row(jnp.tile(hgrn_norm_g[li], N_HEADS)), row(jnp.tile(gdn_norm_g[li], N_HEADS)),
                       row(s5_d[li]), s5_glu_w[li].astype(BF16), row(s5_glu_b[li]),
                       w_out[li].astype(BF16), row(norm2_g[li]), mlp_w1[li].astype(BF16), mlp_w2[li].astype(BF16),
                       row(final_norm_g), nbc, final)
        if final:
            out = res
        else:
            src = (res, res, 0)
    return out.astype(x.dtype)
```
